```python
import math
import jax, jax.numpy as jnp
from jax import lax
import numpy as np

D_MODEL = 1024
BATCH = 8
SEQ = 4096
DEPTH = 1
DEC_BATCH = 32
DEC_SEQ = 16
PAST_LEN = 1024

CHUNK = 64
D_PLE = 256
EPS = 1e-6
SSD_D_INNER = 2 * D_MODEL
SSD_HEAD_DIM = 64
SSD_HEADS = SSD_D_INNER // SSD_HEAD_DIM
SSD_GROUPS = 4
SSD_STATE = 128
SSD_CONV = 4
SSD_CONV_DIM = SSD_D_INNER + 2 * SSD_GROUPS * SSD_STATE
RET_HEADS = 4
RET_DK = 256
RET_DV = 512
RET_QK = RET_HEADS * RET_DK
RET_V = RET_HEADS * RET_DV
ROPE_BASE = 10000.0
D_FF = 2816
FFN_CONV = 3
IN_SIZES = (SSD_D_INNER, SSD_CONV_DIM, SSD_HEADS, RET_QK, RET_QK, RET_V, RET_V, 2 * D_MODEL)
N_IN = SSD_D_INNER + SSD_CONV_DIM + SSD_HEADS + 2 * RET_QK + 2 * RET_V + 2 * D_MODEL

kernel_name = 'hybrid_ssd_retention_streaming_step'


def rmsnorm(x, g):
    xf = x.astype(jnp.float32)
    y = xf * lax.rsqrt(jnp.mean(xf * xf, axis=-1, keepdims=True) + EPS)
    return (y * g.astype(jnp.float32)).astype(x.dtype)


def group_rmsnorm(y, g, groups):
    shp = y.shape
    yf = y.astype(jnp.float32).reshape(*shp[:-1], groups, shp[-1] // groups)
    yf = yf * lax.rsqrt(jnp.mean(yf * yf, axis=-1, keepdims=True) + EPS)
    return (yf.reshape(shp) * g.astype(jnp.float32)).astype(y.dtype)


def head_groupnorm(y, g):
    b, t = y.shape[:2]
    yf = y.astype(jnp.float32)
    mu = jnp.mean(yf, axis=-1, keepdims=True)
    var = jnp.mean(jnp.square(yf - mu), axis=-1, keepdims=True)
    out = ((yf - mu) * lax.rsqrt(var + EPS)).reshape(b, t, -1)
    return (out * g.astype(jnp.float32)).astype(y.dtype)


def causal_dwconv(x, buf, w, b):
    t = x.shape[1]
    width = w.shape[0]
    xp = jnp.concatenate([buf.astype(x.dtype), x], axis=1)
    y = b
    for j in range(width):
        y = y + xp[:, j:j + t] * w[j]
    return y.astype(x.dtype), xp[:, t:]


def rotary(x, pos):
    half = x.shape[-1] // 2
    inv = jnp.power(ROPE_BASE, -jnp.arange(half, dtype=jnp.float32) / half)
    ang = pos.astype(jnp.float32)[:, None] * inv[None, :]
    cos = jnp.cos(ang)[None, :, None, :].astype(x.dtype)
    sin = jnp.sin(ang)[None, :, None, :].astype(x.dtype)
    x1, x2 = x[..., :half], x[..., half:]
    return jnp.concatenate([x1 * cos - x2 * sin, x2 * cos + x1 * sin], axis=-1)


def ssd_block(s0, xs, a_neg):
    x, dt, bm, cm = xs
    bsz, L = x.shape[:2]
    hpg = SSD_HEADS // SSD_GROUPS
    cum = jnp.cumsum(dt * a_neg, axis=1)
    seg = cum[:, :, None, :] - cum[:, None, :, :]
    causal = jnp.tril(jnp.ones((L, L), dtype=bool))
    decay = jnp.exp(jnp.where(causal[None, :, :, None], seg, -jnp.inf))
    xdt = (x * dt[..., None].astype(x.dtype)).reshape(bsz, L, SSD_GROUPS, hpg, SSD_HEAD_DIM)
    cb = jnp.einsum('blgn,bsgn->blsg', cm, bm)
    wts = cb[..., None] * decay.reshape(bsz, L, L, SSD_GROUPS, hpg)
    y = jnp.einsum('blsgh,bsghp->blghp', wts, xdt)
    s0g = s0.reshape(bsz, SSD_GROUPS, hpg, SSD_HEAD_DIM, SSD_STATE)
    y = y + jnp.einsum('blgn,bghpn->blghp', cm, s0g) * jnp.exp(cum).reshape(bsz, L, SSD_GROUPS, hpg)[..., None]
    to_end = jnp.exp(cum[:, -1:, :] - cum).reshape(bsz, L, SSD_GROUPS, hpg)
    s1 = s0g * jnp.exp(cum[:, -1]).reshape(bsz, SSD_GROUPS, hpg)[..., None, None] \
        + jnp.einsum('bsgh,bsghp,bsgn->bghpn', to_end, xdt, bm)
    return s1.reshape(s0.shape).astype(s0.dtype), y.reshape(bsz, L, SSD_HEADS, SSD_HEAD_DIM).astype(x.dtype)


def ret_block(s0, xs, log_gamma):
    q, k, v = xs
    L = q.shape[1]
    idx = jnp.arange(L, dtype=jnp.float32)
    diff = idx[:, None] - idx[None, :]
    dmask = jnp.where(diff[..., None] >= 0, jnp.exp(jnp.maximum(diff, 0.0)[..., None] * log_gamma), 0.0)
    s = jnp.einsum('blhd,bshd->blsh', q, k) * dmask.astype(q.dtype)
    o = jnp.einsum('blsh,bshv->blhv', s, v)
    cross = jnp.exp((idx + 1.0)[:, None] * log_gamma)[None, :, :, None].astype(q.dtype)
    o = o + jnp.einsum('blhd,bhdv->blhv', q, s0.astype(q.dtype)) * cross
    kd = k * jnp.exp((L - 1.0 - idx)[:, None] * log_gamma)[None, :, :, None].astype(k.dtype)
    s1 = s0 * jnp.exp(L * log_gamma)[None, :, None, None].astype(s0.dtype) + jnp.einsum('bshd,bshv->bhdv', kd, v)
    return s1.astype(s0.dtype), o.astype(q.dtype)


def blocked_scan(block_fn, s0, xs):
    t = xs[0].shape[1]
    blk = min(CHUNK, t)
    n = t // blk

    def to_blocks(a):
        return jnp.moveaxis(a.reshape(a.shape[0], n, blk, *a.shape[2:]), 1, 0)

    s1, ys = lax.scan(block_fn, s0, tuple(to_blocks(a) for a in xs))
    ys = jnp.moveaxis(ys, 0, 1)
    return ys.reshape(ys.shape[0], t, *ys.shape[3:]), s1


def _layer(x, p, pos, conv_buf, ssd_s, ret_s, ffn_buf,
           norm1_g, w_in, ssd_conv_w, ssd_conv_b, dt_bias, a_log, d_skip, ssd_norm_g, w_br_ssd,
           ret_norm_g, w_br_ret, gate_b, w_out, norm2_g, w_up, ffn_conv_w, ffn_conv_b, w_down,
           ple_norm_g, w_ple_gate, w_ple_proj):
    bsz, t, _ = x.shape
    h = rmsnorm(x, norm1_g)
    u = h @ w_in
    offs = [int(o) for o in np.cumsum(IN_SIZES)[:-1]]
    z, xbc, dt_raw, q, k, v, g, gates = jnp.split(u, offs, axis=-1)

    xbc, conv_new = causal_dwconv(xbc, conv_buf, ssd_conv_w, ssd_conv_b)
    xbc = jax.nn.silu(xbc)
    xs, bm, cm = jnp.split(xbc, [SSD_D_INNER, SSD_D_INNER + SSD_GROUPS * SSD_STATE], axis=-1)
    xs = xs.reshape(bsz, t, SSD_HEADS, SSD_HEAD_DIM)
    bm = bm.reshape(bsz, t, SSD_GROUPS, SSD_STATE)
    cm = cm.reshape(bsz, t, SSD_GROUPS, SSD_STATE)
    dt = jax.nn.softplus((dt_raw + dt_bias).astype(jnp.float32))
    a_neg = -jnp.exp(a_log.astype(jnp.float32))
    y_ssd, ssd_new = blocked_scan(lambda s, c: ssd_block(s, c, a_neg), ssd_s, (xs, dt, bm, cm))
    y_ssd = (y_ssd + xs * d_skip[:, None]).reshape(bsz, t, SSD_D_INNER)
    y_ssd = group_rmsnorm(y_ssd * jax.nn.silu(z), ssd_norm_g, SSD_GROUPS)

    q = rotary(q.reshape(bsz, t, RET_HEADS, RET_DK), pos)
    k = rotary(k.reshape(bsz, t, RET_HEADS, RET_DK), pos) * (RET_DK ** -0.5)
    v = v.reshape(bsz, t, RET_HEADS, RET_DV)
    log_gamma = jnp.log1p(-jnp.power(2.0, -5.0 - jnp.arange(RET_HEADS, dtype=jnp.float32)))
    y_ret, ret_new = blocked_scan(lambda s, c: ret_block(s, c, log_gamma), ret_s, (q, k, v))
    y_ret = jax.nn.silu(g) * head_groupnorm(y_ret, ret_norm_g)

    g_ssd, g_ret = jnp.split(jax.nn.sigmoid(gates + gate_b), 2, axis=-1)
    mix = g_ssd * (y_ssd @ w_br_ssd) + g_ret * (y_ret @ w_br_ret)
    x = x + mix @ w_out

    up, ffn_new = causal_dwconv(rmsnorm(x, norm2_g) @ w_up, ffn_buf, ffn_conv_w, ffn_conv_b)
    a, b = jnp.split(up, 2, axis=-1)
    x = x + (jax.nn.gelu(a, approximate=False) * b) @ w_down

    x = x + jax.nn.sigmoid(rmsnorm(x, ple_norm_g) @ w_ple_gate) * (p @ w_ple_proj)
    return x, (conv_new, ssd_new, ret_new, ffn_new)


def setup_inputs(seed: int = 0) -> dict:
    key = jax.random.key(seed)
    ks = iter(jax.random.split(key, 40))
    nrm = lambda shape, s=1.0: jax.random.normal(next(ks), shape, dtype=jnp.float32) * s
    gain = lambda shape: 1.0 + nrm(shape, 0.02)
    dt0 = jnp.exp(jax.random.uniform(next(ks), (DEPTH, SSD_HEADS)) * (math.log(0.1) - math.log(0.001)) + math.log(0.001))
    return {
        'x_prompt': nrm((BATCH, SEQ, D_MODEL)),
        'x_sample': nrm((DEC_BATCH, DEC_SEQ, D_MODEL)),
        'p_prompt': nrm((DEPTH, BATCH, SEQ, D_PLE)),
        'p_sample': nrm((DEPTH, DEC_BATCH, DEC_SEQ, D_PLE)),
        'state_ssd_conv': nrm((DEPTH, DEC_BATCH, SSD_CONV - 1, SSD_CONV_DIM)),
        'state_ssd': nrm((DEPTH, DEC_BATCH, SSD_HEADS, SSD_HEAD_DIM, SSD_STATE), 0.1),
        'state_ret': nrm((DEPTH, DEC_BATCH, RET_HEADS, RET_DK, RET_DV), 0.05),
        'state_ffn_conv': nrm((DEPTH, DEC_BATCH, FFN_CONV - 1, 2 * D_FF)),
        'norm1_g': gain((DEPTH, D_MODEL)),
        'w_in': nrm((DEPTH, D_MODEL, N_IN), D_MODEL ** -0.5),
        'ssd_conv_w': nrm((DEPTH, SSD_CONV, SSD_CONV_DIM), SSD_CONV ** -0.5),
        'ssd_conv_b': nrm((DEPTH, SSD_CONV_DIM), 0.02),
        'dt_bias': dt0 + jnp.log(-jnp.expm1(-dt0)),
        'a_log': jnp.log(jax.random.uniform(next(ks), (DEPTH, SSD_HEADS), minval=1.0, maxval=16.0)),
        'd_skip': 1.0 + nrm((DEPTH, SSD_HEADS), 0.1),
        'ssd_norm_g': gain((DEPTH, SSD_D_INNER)),
        'w_br_ssd': nrm((DEPTH, SSD_D_INNER, D_MODEL), SSD_D_INNER ** -0.5),
        'ret_norm_g': gain((DEPTH, RET_V)),
        'w_br_ret': nrm((DEPTH, RET_V, D_MODEL), RET_V ** -0.5),
        'gate_b': nrm((DEPTH, 2 * D_MODEL), 0.02),
        'w_out': nrm((DEPTH, D_MODEL, D_MODEL), D_MODEL ** -0.5),
        'norm2_g': gain((DEPTH, D_MODEL)),
        'w_up': nrm((DEPTH, D_MODEL, 2 * D_FF), D_MODEL ** -0.5),
        'ffn_conv_w': nrm((DEPTH, FFN_CONV, 2 * D_FF), FFN_CONV ** -0.5),
        'ffn_conv_b': nrm((DEPTH, 2 * D_FF), 0.02),
        'w_down': nrm((DEPTH, D_FF, D_MODEL), D_FF ** -0.5),
        'ple_norm_g': gain((DEPTH, D_MODEL)),
        'w_ple_gate': nrm((DEPTH, D_MODEL, D_MODEL), D_MODEL ** -0.5),
        'w_ple_proj': nrm((DEPTH, D_PLE, D_MODEL), D_PLE ** -0.5),
        'final_norm_g': gain((D_MODEL,)),
    }


def reference(x_prompt, x_sample, p_prompt, p_sample, state_ssd_conv, state_ssd, state_ret, state_ffn_conv,
              norm1_g, w_in, ssd_conv_w, ssd_conv_b, dt_bias, a_log, d_skip, ssd_norm_g, w_br_ssd,
              ret_norm_g, w_br_ret, gate_b, w_out, norm2_g, w_up, ffn_conv_w, ffn_conv_b, w_down,
              ple_norm_g, w_ple_gate, w_ple_proj, final_norm_g):
    bp, tp = x_prompt.shape[:2]
    ts = x_sample.shape[1]
    dtype = x_prompt.dtype
    pos_p = jnp.arange(tp)
    pos_s = PAST_LEN + jnp.arange(ts)
    hp, hs = x_prompt, x_sample
    new_p = ([], [], [], [])
    new_s = ([], [], [], [])
    for i in range(DEPTH):
        params = (norm1_g[i], w_in[i], ssd_conv_w[i], ssd_conv_b[i], dt_bias[i], a_log[i], d_skip[i],
                  ssd_norm_g[i], w_br_ssd[i], ret_norm_g[i], w_br_ret[i], gate_b[i], w_out[i], norm2_g[i],
                  w_up[i], ffn_conv_w[i], ffn_conv_b[i], w_down[i], ple_norm_g[i], w_ple_gate[i], w_ple_proj[i])
        zero_conv = jnp.zeros((bp, SSD_CONV - 1, SSD_CONV_DIM), dtype)
        zero_ssd = jnp.zeros((bp, SSD_HEADS, SSD_HEAD_DIM, SSD_STATE), dtype)
        zero_ret = jnp.zeros((bp, RET_HEADS, RET_DK, RET_DV), dtype)
        zero_ffn = jnp.zeros((bp, FFN_CONV - 1, 2 * D_FF), dtype)
        hp, st_p = _layer(hp, p_prompt[i], pos_p, zero_conv, zero_ssd, zero_ret, zero_ffn, *params)
        hs, st_s = _layer(hs, p_sample[i], pos_s, state_ssd_conv[i], state_ssd[i], state_ret[i],
                          state_ffn_conv[i], *params)
        for j in range(4):
            new_p[j].append(st_p[j])
            new_s[j].append(st_s[j])
    y_prompt = rmsnorm(hp, final_norm_g)
    y_sample = rmsnorm(hs, final_norm_g)
    new_ssd_conv_p = jnp.stack(new_p[0])
    new_ssd_p = jnp.stack(new_p[1])
    new_ret_p = jnp.stack(new_p[2])
    new_ffn_conv_p = jnp.stack(new_p[3])
    new_ssd_conv_s = jnp.stack(new_s[0])
    new_ssd_s = jnp.stack(new_s[1])
    new_ret_s = jnp.stack(new_s[2])
    new_ffn_conv_s = jnp.stack(new_s[3])
    return (y_prompt, y_sample, new_ssd_conv_p, new_ssd_p, new_ret_p, new_ffn_conv_p,
            new_ssd_conv_s, new_ssd_s, new_ret_s, new_ffn_conv_s)
```

```python
import functools

import jax
import jax.numpy as jnp
from jax import lax
from jax.experimental import pallas as pl
from jax.experimental.pallas import tpu as pltpu

F32 = jnp.float32
BF16 = jnp.bfloat16

EPS = 1e-6
D_MODEL = 1024
D_PLE = 256
SSD_D_INNER = 2048
SSD_HEAD_DIM = 64
SSD_HEADS = 32
SSD_GROUPS = 4
SSD_STATE = 128
SSD_CONV = 4
SSD_BC = SSD_GROUPS * SSD_STATE
SSD_GROUP_CH = SSD_D_INNER // SSD_GROUPS
HEADS_PER_GROUP = SSD_HEADS // SSD_GROUPS
RET_HEADS = 4
RET_DK = 256
RET_DV = 512
RET_QK = RET_HEADS * RET_DK
RET_V = RET_HEADS * RET_DV
ROPE_BASE = 10000.0
D_FF = 2816
FFN_CONV = 3
PAST_LEN = 1024

U_COLS = 13312
COL_Z, COL_X, COL_V, COL_G, COL_GATES = 0, 2048, 4096, 6144, 8192
COL_Q, COL_K, COL_B, COL_C = 10240, 11264, 12288, 12800
DT_PAD = 128

MIXER_CHUNK = 256
SUBLANES = 8
VMEM_LIMIT = 48 * 1024 * 1024


def _cparams(sem):
    return pltpu.CompilerParams(dimension_semantics=sem, vmem_limit_bytes=VMEM_LIMIT)


def _resident(shape):
    nd = len(shape)
    return pl.BlockSpec(shape, lambda *_: (0,) * nd, pipeline_mode=pl.Buffered(1))


def _rms(x, g):
    return x * lax.rsqrt(jnp.mean(x * x, axis=-1, keepdims=True) + EPS) * g


def _sigmoid(x):
    return 1.0 / (1.0 + jnp.exp(-x))


def _silu(x):
    return x * _sigmoid(x)


def _split3(x):
    hi = x.astype(BF16)
    r1 = x - hi.astype(F32)
    mid = r1.astype(BF16)
    lo = (r1 - mid.astype(F32)).astype(BF16)
    return hi, mid, lo


def _dot(a, b):
    return jnp.dot(a, b, preferred_element_type=F32)


def _dot_nt(a, b):
    return lax.dot_general(a, b, (((1,), (1,)), ((), ())), preferred_element_type=F32)


def _dot_tn(a, b):
    return lax.dot_general(a, b, (((0,), (0,)), ((), ())), preferred_element_type=F32)


def _causal_conv(pre, prev8, w, b):
    width = w.shape[0]
    head = pre[0:SUBLANES]
    row8 = lax.broadcasted_iota(jnp.int32, (SUBLANES, 1), 0)
    acc = b + w[width - 1:width] * pre
    acc8 = b + w[width - 1:width] * head
    for j in range(1, width):
        wj = w[width - 1 - j:width - j]
        acc = acc + wj * pltpu.roll(pre, j, axis=0)
        sh8 = jnp.where(row8 < j, pltpu.roll(prev8, j, axis=0), pltpu.roll(head, j, axis=0))
        acc8 = acc8 + wj * sh8
    if pre.shape[0] == SUBLANES:
        return acc8
    return jnp.concatenate([acc8, acc[SUBLANES:]], axis=0)


def _inproj_kernel(x_ref, g_ref, w_ref, wdt_ref, u_ref, dt_ref, h_ref):
    @pl.when(pl.program_id(1) == 0)
    def _():
        hb = _rms(x_ref[...], g_ref[...]).astype(BF16)
        h_ref[...] = hb
        dt_ref[...] = _dot(hb, wdt_ref[...])

    u_ref[...] = _dot(h_ref[...], w_ref[...]).astype(BF16)


def _in_proj(x2d, g, w, wdt, tm, tn):
    n = x2d.shape[0]
    return pl.pallas_call(
        _inproj_kernel,
        grid=(n // tm, U_COLS // tn),
        in_specs=[
            pl.BlockSpec((tm, D_MODEL), lambda i, j: (i, 0)),
            pl.BlockSpec((1, D_MODEL), lambda i, j: (0, 0)),
            pl.BlockSpec((D_MODEL, tn), lambda i, j: (0, j)),
            pl.BlockSpec((D_MODEL, DT_PAD), lambda i, j: (0, 0)),
        ],
        out_specs=[
            pl.BlockSpec((tm, tn), lambda i, j: (i, j)),
            pl.BlockSpec((tm, DT_PAD), lambda i, j: (i, 0)),
        ],
        out_shape=[
            jax.ShapeDtypeStruct((n, U_COLS), BF16),
            jax.ShapeDtypeStruct((n, DT_PAD), F32),
        ],
        scratch_shapes=[pltpu.VMEM((tm, D_MODEL), BF16)],
        compiler_params=_cparams(("parallel", "arbitrary")),
        name="in_proj",
    )(x2d, g, w, wdt)


def _ssd_kernel(x_ref, b_ref, c_ref, z_ref, dt_ref,
                wx_ref, wb_ref, wc_ref, bx_ref, bb_ref, bc_ref,
                dtb_ref, alog_ref, dsk_ref, ng_ref,
                cx0_ref, cb0_ref, cc0_ref, s0_ref,
                y_ref, cxo_ref, cbo_ref, cco_ref, so_ref,
                s_scr, px_scr, pb_scr, pc_scr, *pad_scr, L, Tv):
    c = pl.program_id(1)
    nc = pl.num_programs(1)

    @pl.when(c == 0)
    def _():
        s_scr[...] = s0_ref[0].reshape(SSD_D_INNER, SSD_STATE)
        px_scr[...] = cx0_ref[0]
        pb_scr[...] = cb0_ref[0]
        pc_scr[...] = cc0_ref[0]

    if Tv < L:
        for scr, ref in zip(pad_scr, (x_ref, b_ref, c_ref, z_ref, dt_ref)):
            scr[...] = jnp.zeros(scr.shape, scr.dtype)
            scr[0:Tv, :] = ref[...]
        x_in, b_in, c_in, z_in, dt_in = pad_scr
    else:
        x_in, b_in, c_in, z_in, dt_in = x_ref, b_ref, c_ref, z_ref, dt_ref

    rows = lax.broadcasted_iota(jnp.int32, (L, 1), 0)
    dtv = dt_in[...] + dtb_ref[...]
    dt = jnp.maximum(dtv, 0.0) + jnp.log(1.0 + jnp.exp(-jnp.abs(dtv)))
    if Tv < L:
        dt = jnp.where(rows < Tv, dt, 0.0)
    dA = dt * (-jnp.exp(alog_ref[...]))

    li = lax.broadcasted_iota(jnp.int32, (L, L), 0)
    si = lax.broadcasted_iota(jnp.int32, (L, L), 1)
    causal = li >= si
    tri = jnp.where(causal, 1.0, 0.0).astype(BF16)
    hi, mid, lo = _split3(dA)
    cum = _dot(tri, hi) + _dot(tri, mid) + _dot(tri, lo)
    cum_t = cum.T
    dt_t = dt.T
    cum_last = cum[Tv - 1:Tv, :]
    ecum = jnp.exp(cum)
    todt = jnp.exp(cum_last - cum) * dt
    dec_tot = jnp.exp(cum_last)

    lane = lax.broadcasted_iota(jnp.int32, (1, 2 * SSD_HEAD_DIM), 1)
    lo_half = lane < SSD_HEAD_DIM
    P2 = 2 * SSD_HEAD_DIM

    for g in range(SSD_GROUPS):
        gx = slice(g * SSD_GROUP_CH, (g + 1) * SSD_GROUP_CH)
        gn = slice(g * SSD_STATE, (g + 1) * SSD_STATE)
        xpre = x_in[:, gx].astype(F32)
        bpre = b_in[:, gn].astype(F32)
        cpre = c_in[:, gn].astype(F32)
        xc = _silu(_causal_conv(xpre, px_scr[:, gx], wx_ref[:, gx], bx_ref[:, gx]))
        bc = _silu(_causal_conv(bpre, pb_scr[:, gn], wb_ref[:, gn], bb_ref[:, gn]))
        cc = _silu(_causal_conv(cpre, pc_scr[:, gn], wc_ref[:, gn], bc_ref[:, gn]))
        px_scr[:, gx] = xpre[Tv - SUBLANES:Tv]
        pb_scr[:, gn] = bpre[Tv - SUBLANES:Tv]
        pc_scr[:, gn] = cpre[Tv - SUBLANES:Tv]

        xcb = xc.astype(BF16)
        bcb = bc.astype(BF16)
        ccb = cc.astype(BF16)
        cb = _dot_nt(ccb, bcb)
        s_old = s_scr[gx, :]
        y_inter = _dot_nt(ccb, s_old.astype(BF16))

        y_parts = []
        xw_parts = []
        for pr in range(HEADS_PER_GROUP // 2):
            w_pair, e_cols, t_cols = [], [], []
            for k in range(2):
                h = g * HEADS_PER_GROUP + 2 * pr + k
                seg = cum[:, h:h + 1] - cum_t[h:h + 1, :]
                w = jnp.where(causal, jnp.exp(seg), 0.0) * cb * dt_t[h:h + 1, :]
                w_pair.append(w.astype(BF16))
                e_cols.append(ecum[:, h:h + 1])
                t_cols.append(todt[:, h:h + 1])
            ps = slice(pr * P2, (pr + 1) * P2)
            x_pair = xc[:, ps]
            x_pair_b = xcb[:, ps]
            zero = jnp.zeros_like(x_pair_b)
            rhs = jnp.concatenate([jnp.where(lo_half, x_pair_b, zero),
                                   jnp.where(lo_half, zero, x_pair_b)], axis=0)
            y_pair = _dot(jnp.concatenate(w_pair, axis=1), rhs)
            y_pair = y_pair + y_inter[:, ps] * jnp.where(lo_half, e_cols[0], e_cols[1])
            y_pair = y_pair + x_pair * dsk_ref[:, g * SSD_GROUP_CH + pr * P2:g * SSD_GROUP_CH + (pr + 1) * P2]
            y_parts.append(y_pair)
            xw_parts.append((x_pair * jnp.where(lo_half, t_cols[0], t_cols[1])).astype(BF16))

        y = jnp.concatenate(y_parts, axis=1)
        xw = jnp.concatenate(xw_parts, axis=1)

        dec_rows = [jnp.broadcast_to(dec_tot[:, g * HEADS_PER_GROUP + k:g * HEADS_PER_GROUP + k + 1],
                                     (SSD_HEAD_DIM, 1)) for k in range(HEADS_PER_GROUP)]
        s_new = s_old * jnp.concatenate(dec_rows, axis=0) + _dot_tn(xw, bcb)
        s_scr[gx, :] = s_new

        yz = y * _silu(z_in[:, gx].astype(F32))
        yn = yz * lax.rsqrt(jnp.mean(yz * yz, axis=-1, keepdims=True) + EPS) * ng_ref[:, gx]
        y_ref[:, gx] = yn[0:Tv].astype(BF16)

    @pl.when(c == nc - 1)
    def _():
        so_ref[0] = s_scr[...].reshape(SSD_HEADS, SSD_HEAD_DIM, SSD_STATE)
        cxo_ref[0] = px_scr[...]
        cbo_ref[0] = pb_scr[...]
        cco_ref[0] = pc_scr[...]


def _ssd(u, dt_raw, prm, cx0, cb0, cc0, s0, nb, t, L, Tv):
    nc = t // Tv
    rb = lambda b, c: b * nc + c
    full = lambda w: pl.BlockSpec(w.shape, lambda b, c: (0,) * w.ndim)
    per_b = lambda shp: pl.BlockSpec((1,) + shp, lambda b, c: (b,) + (0,) * len(shp))
    names = ["cw_x", "cw_b", "cw_c", "cb_x", "cb_b", "cb_c", "dt_bias", "a_log", "d_skip", "ssd_norm_g"]
    in_specs = [
        pl.BlockSpec((Tv, SSD_D_INNER), lambda b, c: (rb(b, c), COL_X // SSD_D_INNER)),
        pl.BlockSpec((Tv, SSD_BC), lambda b, c: (rb(b, c), COL_B // SSD_BC)),
        pl.BlockSpec((Tv, SSD_BC), lambda b, c: (rb(b, c), COL_C // SSD_BC)),
        pl.BlockSpec((Tv, SSD_D_INNER), lambda b, c: (rb(b, c), COL_Z // SSD_D_INNER)),
        pl.BlockSpec((Tv, DT_PAD), lambda b, c: (rb(b, c), 0)),
    ] + [full(prm[k]) for k in names] + [
        per_b((SUBLANES, SSD_D_INNER)), per_b((SUBLANES, SSD_BC)), per_b((SUBLANES, SSD_BC)),
        per_b((SSD_HEADS, SSD_HEAD_DIM, SSD_STATE)),
    ]
    out_specs = [
        pl.BlockSpec((Tv, SSD_D_INNER), lambda b, c: (rb(b, c), 0)),
        per_b((SUBLANES, SSD_D_INNER)), per_b((SUBLANES, SSD_BC)), per_b((SUBLANES, SSD_BC)),
        per_b((SSD_HEADS, SSD_HEAD_DIM, SSD_STATE)),
    ]
    out_shape = [
        jax.ShapeDtypeStruct((nb * t, SSD_D_INNER), BF16),
        jax.ShapeDtypeStruct((nb, SUBLANES, SSD_D_INNER), F32),
        jax.ShapeDtypeStruct((nb, SUBLANES, SSD_BC), F32),
        jax.ShapeDtypeStruct((nb, SUBLANES, SSD_BC), F32),
        jax.ShapeDtypeStruct((nb, SSD_HEADS, SSD_HEAD_DIM, SSD_STATE), F32),
    ]
    scratch = [
        pltpu.VMEM((SSD_D_INNER, SSD_STATE), F32),
        pltpu.VMEM((SUBLANES, SSD_D_INNER), F32),
        pltpu.VMEM((SUBLANES, SSD_BC), F32),
        pltpu.VMEM((SUBLANES, SSD_BC), F32),
    ]
    if Tv < L:
        scratch += [
            pltpu.VMEM((L, SSD_D_INNER), BF16), pltpu.VMEM((L, SSD_BC), BF16),
            pltpu.VMEM((L, SSD_BC), BF16), pltpu.VMEM((L, SSD_D_INNER), BF16),
            pltpu.VMEM((L, DT_PAD), F32),
        ]
    return pl.pallas_call(
        functools.partial(_ssd_kernel, L=L, Tv=Tv),
        grid=(nb, nc),
        in_specs=in_specs, out_specs=out_specs, out_shape=out_shape,
        scratch_shapes=scratch,
        compiler_params=_cparams(("parallel", "arbitrary")),
        name="ssd",
    )(u, u, u, u, dt_raw, *[prm[k] for k in names], cx0, cb0, cc0, s0)


def _ret_kernel(q_ref, k_ref, v_ref, g_ref, cos_ref, sin_ref, lg_ref, ng_ref, s0_ref,
                y_ref, so_ref, s_scr, *pad_scr, L, Tv):
    c = pl.program_id(2)
    nc = pl.num_programs(2)

    @pl.when(c == 0)
    def _():
        s_scr[...] = s0_ref[0, 0]

    if Tv < L:
        for scr, ref in zip(pad_scr, (q_ref, k_ref, v_ref, g_ref, cos_ref, sin_ref)):
            scr[...] = jnp.zeros(scr.shape, scr.dtype)
            scr[0:Tv, :] = ref[...]
        q_in, k_in, v_in, g_in, cos_in, sin_in = pad_scr
    else:
        q_in, k_in, v_in, g_in, cos_in, sin_in = q_ref, k_ref, v_ref, g_ref, cos_ref, sin_ref

    half = RET_DK // 2
    cos = cos_in[...]
    sin = sin_in[...]

    def rot(ref):
        xf = ref[...].astype(F32)
        x1, x2 = xf[:, :half], xf[:, half:]
        return jnp.concatenate([x1 * cos - x2 * sin, x2 * cos + x1 * sin], axis=1)

    lg = lg_ref[0][:, 0:1]
    qr = rot(q_in).astype(BF16)
    kr = rot(k_in) * (RET_DK ** -0.5)
    rowf = lax.broadcasted_iota(jnp.int32, (L, 1), 0).astype(F32)
    colf = lax.broadcasted_iota(jnp.int32, (1, L), 1).astype(F32)
    diff = rowf - colf
    dmask = jnp.where(diff >= 0.0, jnp.exp(jnp.maximum(diff, 0.0) * lg), 0.0)
    vb = v_in[...]
    s_old = s_scr[...]
    sc = (_dot_nt(qr, kr.astype(BF16)) * dmask).astype(BF16)
    o = _dot(sc, vb) + _dot(qr, s_old.astype(BF16)) * jnp.exp((rowf + 1.0) * lg)
    kd = kr * jnp.exp((Tv - 1.0 - rowf) * lg)
    if Tv < L:
        kd = jnp.where(rowf < Tv, kd, 0.0)
    s_new = s_old * jnp.exp(Tv * lg) + _dot_tn(kd.astype(BF16), vb)
    s_scr[...] = s_new

    mu = jnp.mean(o, axis=-1, keepdims=True)
    d = o - mu
    var = jnp.mean(d * d, axis=-1, keepdims=True)
    yn = d * lax.rsqrt(var + EPS) * ng_ref[...]
    y = _silu(g_in[...].astype(F32)) * yn
    y_ref[...] = y[0:Tv].astype(BF16)

    @pl.when(c == nc - 1)
    def _():
        so_ref[0, 0] = s_new


def _ret(u, cos, sin, lg, ng, s0, nb, t, L, Tv):
    nc = t // Tv
    rb = lambda b, h, c: b * nc + c
    in_specs = [
        pl.BlockSpec((Tv, RET_DK), lambda b, h, c: (rb(b, h, c), COL_Q // RET_DK + h)),
        pl.BlockSpec((Tv, RET_DK), lambda b, h, c: (rb(b, h, c), COL_K // RET_DK + h)),
        pl.BlockSpec((Tv, RET_DV), lambda b, h, c: (rb(b, h, c), COL_V // RET_DV + h)),
        pl.BlockSpec((Tv, RET_DV), lambda b, h, c: (rb(b, h, c), COL_G // RET_DV + h)),
        pl.BlockSpec((Tv, RET_DK // 2), lambda b, h, c: (c, 0)),
        pl.BlockSpec((Tv, RET_DK // 2), lambda b, h, c: (c, 0)),
        pl.BlockSpec((1, 1, 128), lambda b, h, c: (h, 0, 0)),
        pl.BlockSpec((1, RET_DV), lambda b, h, c: (0, h)),
        pl.BlockSpec((1, 1, RET_DK, RET_DV), lambda b, h, c: (b, h, 0, 0)),
    ]
    out_specs = [
        pl.BlockSpec((Tv, RET_DV), lambda b, h, c: (rb(b, h, c), h)),
        pl.BlockSpec((1, 1, RET_DK, RET_DV), lambda b, h, c: (b, h, 0, 0)),
    ]
    out_shape = [
        jax.ShapeDtypeStruct((nb * t, RET_V), BF16),
        jax.ShapeDtypeStruct((nb, RET_HEADS, RET_DK, RET_DV), F32),
    ]
    scratch = [pltpu.VMEM((RET_DK, RET_DV), F32)]
    if Tv < L:
        scratch += [
            pltpu.VMEM((L, RET_DK), BF16), pltpu.VMEM((L, RET_DK), BF16),
            pltpu.VMEM((L, RET_DV), BF16), pltpu.VMEM((L, RET_DV), BF16),
            pltpu.VMEM((L, RET_DK // 2), F32), pltpu.VMEM((L, RET_DK // 2), F32),
        ]
    return pl.pallas_call(
        functools.partial(_ret_kernel, L=L, Tv=Tv),
        grid=(nb, RET_HEADS, nc),
        in_specs=in_specs, out_specs=out_specs, out_shape=out_shape,
        scratch_shapes=scratch,
        compiler_params=_cparams(("parallel", "arbitrary", "arbitrary")),
        name="retention",
    )(u, u, u, u, cos, sin, lg, ng, s0)


def _merge_kernel(x_ref, ys_ref, yr_ref, gt_ref, gb_ref, ws_ref, wr_ref, wo_ref, o_ref):
    gates = _sigmoid(gt_ref[...].astype(F32) + gb_ref[...])
    a = _dot(ys_ref[...], ws_ref[...])
    b = _dot(yr_ref[...], wr_ref[...])
    mix = gates[:, :D_MODEL] * a + gates[:, D_MODEL:] * b
    o_ref[...] = x_ref[...] + _dot(mix.astype(BF16), wo_ref[...])


def _merge(x2d, y_ssd, y_ret, u, gate_b, w_s, w_r, w_o, tm):
    n = x2d.shape[0]
    return pl.pallas_call(
        _merge_kernel,
        grid=(n // tm,),
        in_specs=[
            pl.BlockSpec((tm, D_MODEL), lambda i: (i, 0)),
            pl.BlockSpec((tm, SSD_D_INNER), lambda i: (i, 0)),
            pl.BlockSpec((tm, RET_V), lambda i: (i, 0)),
            pl.BlockSpec((tm, 2 * D_MODEL), lambda i: (i, COL_GATES // (2 * D_MODEL))),
            _resident(gate_b.shape), _resident(w_s.shape), _resident(w_r.shape), _resident(w_o.shape),
        ],
        out_specs=pl.BlockSpec((tm, D_MODEL), lambda i: (i, 0)),
        out_shape=jax.ShapeDtypeStruct((n, D_MODEL), F32),
        compiler_params=_cparams(("parallel",)),
        name="merge",
    )(x2d, y_ssd, y_ret, u, gate_b, w_s, w_r, w_o)


FFN_CHUNK = 256
FFN_NCHUNK = D_FF // FFN_CHUNK
FFN_SHORT_TILE = 128


def _ffn_kernel(x_ref, p_ref, n2_ref, wup_ref, cw_ref, cb_ref, wdn_ref, pg_ref, wpg_ref, wpp_ref,
                fg_ref, *rest, tm, seq_rows):
    multi_seq = seq_rows < tm
    if multi_seq:
        e1_ref, e2_ref, y_ref, up_ref, act_scr = rest
        tpos = lax.rem(lax.broadcasted_iota(jnp.int32, (tm, 1), 0), seq_rows)
    else:
        c0_ref, y_ref, co_ref, act_scr, carry_scr = rest
        i = pl.program_id(1)

        @pl.when(i == 0)
        def _():
            carry_scr[...] = c0_ref[0]

    x1 = x_ref[...]
    h2 = _rms(x1, n2_ref[...]).astype(BF16)

    for cc in range(FFN_NCHUNK):
        halves = []
        for base in (0, D_FF):
            sl = slice(base + cc * FFN_CHUNK, base + (cc + 1) * FFN_CHUNK)
            up = _dot(h2, wup_ref[:, sl])
            if multi_seq:
                up_ref[:, sl] = up
                sh1 = jnp.where(tpos == 0, e1_ref[:, sl], pltpu.roll(up, 1, axis=0))
                sh2 = jnp.where(tpos < 2, e2_ref[:, sl], pltpu.roll(up, 2, axis=0))
                conv = (cb_ref[:, sl] + cw_ref[2:3, sl] * up + cw_ref[1:2, sl] * sh1
                        + cw_ref[0:1, sl] * sh2)
            else:
                conv = _causal_conv(up, carry_scr[:, sl], cw_ref[:, sl], cb_ref[:, sl])
                carry_scr[:, sl] = up[tm - SUBLANES:tm]
            halves.append(conv)
        a, b = halves
        gelu = 0.5 * a * (1.0 + lax.erf(a * (2.0 ** -0.5)))
        act_scr[:, cc * FFN_CHUNK:(cc + 1) * FFN_CHUNK] = (gelu * b).astype(BF16)

    x2 = x1 + _dot(act_scr[...], wdn_ref[...])
    hg = _rms(x2, pg_ref[...]).astype(BF16)
    gate = _sigmoid(_dot(hg, wpg_ref[...]))
    x3 = x2 + gate * _dot(p_ref[...].astype(BF16), wpp_ref[...])
    y_ref[...] = _rms(x3, fg_ref[...])

    if not multi_seq:
        @pl.when(i == pl.num_programs(1) - 1)
        def _():
            co_ref[0] = carry_scr[...]


def _ffn(x1, p2d, prm, hist, nb, t, tm):
    n = nb * t
    multi_seq = t < tm
    weights = [prm[k] for k in ("norm2_g", "w_up", "ffn_conv_w", "ffn_conv_b", "w_down", "ple_norm_g",
                                "w_ple_gate", "w_ple_proj", "final_norm_g")]
    wspecs = [_resident(w.shape) for w in weights]
    if multi_seq:
        assert tm % t == 0 and n % tm == 0
        grid = (n // tm,)
        row = lambda i: (i, 0)
        in_specs = ([pl.BlockSpec((tm, D_MODEL), row), pl.BlockSpec((tm, D_PLE), row)] + wspecs
                    + [pl.BlockSpec((tm, 2 * D_FF), row), pl.BlockSpec((tm, 2 * D_FF), row)])
        out_specs = [pl.BlockSpec((tm, D_MODEL), row), pl.BlockSpec((tm, 2 * D_FF), row)]
        out_shape = [jax.ShapeDtypeStruct((n, D_MODEL), F32), jax.ShapeDtypeStruct((n, 2 * D_FF), F32)]
        scratch = [pltpu.VMEM((tm, D_FF), BF16)]
        sem = ("parallel",)
        args = (x1, p2d, *weights, *hist)
    else:
        nt = t // tm
        grid = (nb, nt)
        row = lambda b, i: (b * nt + i, 0)
        in_specs = ([pl.BlockSpec((tm, D_MODEL), row), pl.BlockSpec((tm, D_PLE), row)] + wspecs
                    + [pl.BlockSpec((1, SUBLANES, 2 * D_FF), lambda b, i: (b, 0, 0))])
        out_specs = [pl.BlockSpec((tm, D_MODEL), row),
                     pl.BlockSpec((1, SUBLANES, 2 * D_FF), lambda b, i: (b, 0, 0))]
        out_shape = [jax.ShapeDtypeStruct((n, D_MODEL), F32),
                     jax.ShapeDtypeStruct((nb, SUBLANES, 2 * D_FF), F32)]
        scratch = [pltpu.VMEM((tm, D_FF), BF16), pltpu.VMEM((SUBLANES, 2 * D_FF), F32)]
        sem = ("parallel", "arbitrary")
        args = (x1, p2d, *weights, hist)
    return pl.pallas_call(
        functools.partial(_ffn_kernel, tm=tm, seq_rows=t),
        grid=grid, in_specs=in_specs, out_specs=out_specs, out_shape=out_shape,
        scratch_shapes=scratch, compiler_params=_cparams(sem), name="ffn",
    )(*args)


def _prep_params(norm1_g, w_in, ssd_conv_w, ssd_conv_b, dt_bias, a_log, d_skip, ssd_norm_g, w_br_ssd,
                 ret_norm_g, w_br_ret, gate_b, w_out, norm2_g, w_up, ffn_conv_w, ffn_conv_b, w_down,
                 ple_norm_g, w_ple_gate, w_ple_proj, final_norm_g):
    o = {}
    sizes = (SSD_D_INNER, SSD_D_INNER, SSD_BC, SSD_BC, SSD_HEADS, RET_QK, RET_QK, RET_V, RET_V, 2 * D_MODEL)
    offs = [0]
    for s in sizes:
        offs.append(offs[-1] + s)
    wz, wx, wb, wc, wdt, wq, wk, wv, wg, wgt = [w_in[:, offs[i]:offs[i + 1]] for i in range(len(sizes))]
    o["w_u"] = jnp.concatenate([wz, wx, wv, wg, wgt, wq, wk, wb, wc], axis=1).astype(BF16)
    o["w_dt"] = jnp.pad(wdt, ((0, 0), (0, DT_PAD - SSD_HEADS))).astype(BF16)
    o["norm1_g"] = norm1_g.reshape(1, -1)
    o["cw_x"] = ssd_conv_w[:, :SSD_D_INNER]
    o["cw_b"] = ssd_conv_w[:, SSD_D_INNER:SSD_D_INNER + SSD_BC]
    o["cw_c"] = ssd_conv_w[:, SSD_D_INNER + SSD_BC:]
    cb = ssd_conv_b.reshape(1, -1)
    o["cb_x"] = cb[:, :SSD_D_INNER]
    o["cb_b"] = cb[:, SSD_D_INNER:SSD_D_INNER + SSD_BC]
    o["cb_c"] = cb[:, SSD_D_INNER + SSD_BC:]
    o["dt_bias"] = jnp.pad(dt_bias.reshape(1, -1), ((0, 0), (0, DT_PAD - SSD_HEADS)))
    o["a_log"] = jnp.pad(a_log.reshape(1, -1), ((0, 0), (0, DT_PAD - SSD_HEADS)))
    o["d_skip"] = jnp.repeat(d_skip, SSD_HEAD_DIM).reshape(1, -1)
    o["ssd_norm_g"] = ssd_norm_g.reshape(1, -1)
    o["w_br_ssd"] = w_br_ssd.astype(BF16)
    o["ret_norm_g"] = ret_norm_g.reshape(1, -1)
    o["w_br_ret"] = w_br_ret.astype(BF16)
    o["gate_b"] = gate_b.reshape(1, -1)
    o["w_out"] = w_out.astype(BF16)
    o["norm2_g"] = norm2_g.reshape(1, -1)
    o["w_up"] = w_up.astype(BF16)
    o["ffn_conv_w"] = ffn_conv_w
    o["ffn_conv_b"] = ffn_conv_b.reshape(1, -1)
    o["w_down"] = w_down.astype(BF16)
    o["ple_norm_g"] = ple_norm_g.reshape(1, -1)
    o["w_ple_gate"] = w_ple_gate.astype(BF16)
    o["w_ple_proj"] = w_ple_proj.astype(BF16)
    o["final_norm_g"] = final_norm_g.reshape(1, -1)
    lg = jnp.log1p(-jnp.power(2.0, -5.0 - jnp.arange(RET_HEADS, dtype=F32)))
    o["log_gamma"] = jnp.broadcast_to(lg[:, None, None], (RET_HEADS, 1, 128))
    return o


def _pad_tail(buf):
    return jnp.pad(buf, ((0, 0), (SUBLANES - buf.shape[1], 0), (0, 0)))


def _group(x, p, pos0, conv_buf, ssd_s, ret_s, ffn_buf, prm, chunk):
    nb, t, _ = x.shape
    n = nb * t
    x2d = x.reshape(n, D_MODEL)
    p2d = p.reshape(n, D_PLE)

    u, dt_raw = _in_proj(x2d, prm["norm1_g"], prm["w_u"], prm["w_dt"], min(1024, n), 1024)

    Tv = min(t, chunk)
    L = max(Tv, 128)

    cx0 = _pad_tail(conv_buf[:, :, :SSD_D_INNER])
    cb0 = _pad_tail(conv_buf[:, :, SSD_D_INNER:SSD_D_INNER + SSD_BC])
    cc0 = _pad_tail(conv_buf[:, :, SSD_D_INNER + SSD_BC:])
    y_ssd, cxo, cbo, cco, ssd_new = _ssd(u, dt_raw, prm, cx0, cb0, cc0, ssd_s, nb, t, L, Tv)
    conv_new = jnp.concatenate([cxo, cbo, cco], axis=-1)[:, SUBLANES - (SSD_CONV - 1):, :]

    half = RET_DK // 2
    inv = jnp.power(ROPE_BASE, -jnp.arange(half, dtype=F32) / half)
    ang = (pos0 + jnp.arange(t)).astype(F32)[:, None] * inv[None, :]
    y_ret, ret_new = _ret(u, jnp.cos(ang), jnp.sin(ang), prm["log_gamma"], prm["ret_norm_g"], ret_s,
                          nb, t, L, Tv)

    tm = min(512, n)
    x1 = _merge(x2d, y_ssd, y_ret, u, prm["gate_b"], prm["w_br_ssd"], prm["w_br_ret"], prm["w_out"], tm)

    if t >= tm:
        y, ffn_tail = _ffn(x1, p2d, prm, _pad_tail(ffn_buf), nb, t, tm)
        ffn_new = ffn_tail[:, SUBLANES - (FFN_CONV - 1):, :]
    else:
        z = jnp.zeros((nb, t, 2 * D_FF), F32)
        e1 = z.at[:, 0].set(ffn_buf[:, 1]).reshape(n, 2 * D_FF)
        e2 = z.at[:, 0].set(ffn_buf[:, 0]).at[:, 1].set(ffn_buf[:, 1]).reshape(n, 2 * D_FF)
        y, up = _ffn(x1, p2d, prm, (e1, e2), nb, t, min(FFN_SHORT_TILE, n))
        ffn_new = up.reshape(nb, t, 2 * D_FF)[:, t - (FFN_CONV - 1):, :]
    return y.reshape(nb, t, D_MODEL), conv_new, ssd_new, ret_new, ffn_new


def kernel(x_prompt, x_sample, p_prompt, p_sample, state_ssd_conv, state_ssd, state_ret, state_ffn_conv,
           norm1_g, w_in, ssd_conv_w, ssd_conv_b, dt_bias, a_log, d_skip, ssd_norm_g, w_br_ssd,
           ret_norm_g, w_br_ret, gate_b, w_out, norm2_g, w_up, ffn_conv_w, ffn_conv_b, w_down,
           ple_norm_g, w_ple_gate, w_ple_proj, final_norm_g):
    assert norm1_g.shape[0] == 1, "single-layer model"
    chunk = MIXER_CHUNK
    prm = _prep_params(norm1_g[0], w_in[0], ssd_conv_w[0], ssd_conv_b[0], dt_bias[0], a_log[0], d_skip[0],
                       ssd_norm_g[0], w_br_ssd[0], ret_norm_g[0], w_br_ret[0], gate_b[0], w_out[0],
                       norm2_g[0], w_up[0], ffn_conv_w[0], ffn_conv_b[0], w_down[0], ple_norm_g[0],
                       w_ple_gate[0], w_ple_proj[0], final_norm_g)
    bp = x_prompt.shape[0]
    dtype = x_prompt.dtype
    zero_conv = jnp.zeros((bp, SSD_CONV - 1, SSD_D_INNER + 2 * SSD_BC), dtype)
    zero_ssd = jnp.zeros((bp, SSD_HEADS, SSD_HEAD_DIM, SSD_STATE), dtype)
    zero_ret = jnp.zeros((bp, RET_HEADS, RET_DK, RET_DV), dtype)
    zero_ffn = jnp.zeros((bp, FFN_CONV - 1, 2 * D_FF), dtype)
    yp, cp, sp, rp, fp = _group(x_prompt, p_prompt[0], 0, zero_conv, zero_ssd, zero_ret, zero_ffn, prm, chunk)
    ys, cs, ss, rs, fs = _group(x_sample, p_sample[0], PAST_LEN, state_ssd_conv[0], state_ssd[0],
                                state_ret[0], state_ffn_conv[0], prm, chunk)
    return (yp, ys, cp[None], sp[None], rp[None], fp[None], cs[None], ss[None], rs[None], fs[None])
```

```python
import functools

import jax
import jax.numpy as jnp
from jax import lax
from jax.experimental import pallas as pl
from jax.experimental.pallas import tpu as pltpu

F32 = jnp.float32
BF16 = jnp.bfloat16

EPS = 1e-6
D_MODEL = 1024
D_PLE = 256
SSD_D_INNER = 2048
SSD_HEAD_DIM = 64
SSD_HEADS = 32
SSD_GROUPS = 4
SSD_STATE = 128
SSD_CONV = 4
SSD_BC = SSD_GROUPS * SSD_STATE
SSD_GROUP_CH = SSD_D_INNER // SSD_GROUPS
HEADS_PER_GROUP = SSD_HEADS // SSD_GROUPS
RET_HEADS = 4
RET_DK = 256
RET_DV = 512
RET_QK = RET_HEADS * RET_DK
RET_V = RET_HEADS * RET_DV
ROPE_BASE = 10000.0
D_FF = 2816
FFN_CONV = 3
PAST_LEN = 1024

U_TILE = 1024
U_COLS = 13 * U_TILE
COL_Z, COL_X, COL_V, COL_G, COL_GATES = 0, 2048, 4096, 6144, 8192
COL_Q, COL_K, COL_B, COL_C = 10240, 11264, 12288, 12800
TILE_Q, TILE_K = COL_Q // U_TILE, COL_K // U_TILE
DT_PAD = 128

LANES = 128
SUBLANES = 8
MIXER_CHUNK = 256
MIXER_MIN_ROWS = 128
VMEM_LIMIT = 56 * 1024 * 1024
LOG2E = 1.4426950408889634


def _cparams(sem):
    return pltpu.CompilerParams(dimension_semantics=sem, vmem_limit_bytes=VMEM_LIMIT)


def _resident(shape):
    nd = len(shape)
    return pl.BlockSpec(shape, lambda *_: (0,) * nd, pipeline_mode=pl.Buffered(1))


def _rms(x, g):
    return x * lax.rsqrt(jnp.mean(x * x, axis=-1, keepdims=True) + EPS) * g


def _sigmoid(x):
    return 1.0 / (1.0 + jnp.exp(-x))


def _silu(x):
    return x * _sigmoid(x)


def _split3(x):
    hi = x.astype(BF16)
    r1 = x - hi.astype(F32)
    mid = r1.astype(BF16)
    lo = (r1 - mid.astype(F32)).astype(BF16)
    return hi, mid, lo


def _dot(a, b):
    return jnp.dot(a, b, preferred_element_type=F32)


def _dot_nt(a, b):
    return lax.dot_general(a, b, (((1,), (1,)), ((), ())), preferred_element_type=F32)


def _dot_tn(a, b):
    return lax.dot_general(a, b, (((0,), (0,)), ((), ())), preferred_element_type=F32)


def _causal_conv(pre, prev8s, w, b):
    width = w.shape[0]
    seg = pre.shape[0] // len(prev8s)
    row8 = lax.broadcasted_iota(jnp.int32, (SUBLANES, 1), 0)
    taps = [w[width - 1 - j:width - j] for j in range(width)]
    acc = b + taps[0] * pre
    for j in range(1, width):
        acc = acc + taps[j] * pltpu.roll(pre, j, axis=0)
    pieces = []
    for s, prev8 in enumerate(prev8s):
        head = pre[s * seg:s * seg + SUBLANES]
        acc8 = b + taps[0] * head
        for j in range(1, width):
            sh8 = jnp.where(row8 < j, pltpu.roll(prev8, j, axis=0), pltpu.roll(head, j, axis=0))
            acc8 = acc8 + taps[j] * sh8
        pieces.append(acc8)
        if seg > SUBLANES:
            pieces.append(acc[s * seg + SUBLANES:(s + 1) * seg])
    return pieces[0] if len(pieces) == 1 else jnp.concatenate(pieces, axis=0)


INPROJ_SUB = 512


def _inproj_kernel(x_ref, g_ref, w_ref, wdt_ref, gb_ref, cos_ref, sin_ref, u_ref, dt_ref, h_ref, *, tm):
    j = pl.program_id(1)
    sub = min(INPROJ_SUB, tm)

    @pl.when(j == 0)
    def _():
        for r in range(tm // sub):
            rs = slice(r * sub, (r + 1) * sub)
            hb = _rms(x_ref[rs, :], g_ref[...]).astype(BF16)
            h_ref[rs, :] = hb
            dt_ref[rs, :] = _dot(hb, wdt_ref[...])

    def tiles(epilogue):
        for r in range(tm // sub):
            rs = slice(r * sub, (r + 1) * sub)
            u_ref[rs, :] = epilogue(_dot(h_ref[rs, :], w_ref[...]), rs).astype(BF16)

    def rotary(scale):
        def ep(acc, rs):
            cos = cos_ref[rs, :]
            sin = sin_ref[rs, :]
            half = RET_DK // 2
            out = []
            for h in range(U_TILE // RET_DK):
                x1 = acc[:, h * RET_DK:h * RET_DK + half]
                x2 = acc[:, h * RET_DK + half:(h + 1) * RET_DK]
                out += [(x1 * cos - x2 * sin) * scale, (x2 * cos + x1 * sin) * scale]
            return jnp.concatenate(out, axis=1)
        return ep

    is_silu = (j < COL_X // U_TILE) | ((j >= COL_G // U_TILE) & (j < COL_GATES // U_TILE))
    is_gate = (j >= COL_GATES // U_TILE) & (j < TILE_Q)
    is_plain = jnp.logical_not(is_silu | is_gate | (j == TILE_Q) | (j == TILE_K))

    @pl.when(is_plain)
    def _():
        tiles(lambda acc, rs: acc)

    @pl.when(is_silu)
    def _():
        tiles(lambda acc, rs: _silu(acc))

    @pl.when(is_gate)
    def _():
        tiles(lambda acc, rs: _sigmoid(acc + gb_ref[...]))

    @pl.when(j == TILE_Q)
    def _():
        tiles(rotary(1.0))

    @pl.when(j == TILE_K)
    def _():
        tiles(rotary(RET_DK ** -0.5))


def _in_proj(x2d, prm, cos, sin, tm):
    n = x2d.shape[0]
    npos = cos.shape[0] // tm
    gate_tile0 = COL_GATES // U_TILE
    gb_map = lambda i, j: (0, jnp.clip(j - gate_tile0, 0, 2 * D_MODEL // U_TILE - 1))
    return pl.pallas_call(
        functools.partial(_inproj_kernel, tm=tm),
        grid=(n // tm, U_COLS // U_TILE),
        in_specs=[
            pl.BlockSpec((tm, D_MODEL), lambda i, j: (i, 0)),
            pl.BlockSpec((1, D_MODEL), lambda i, j: (0, 0)),
            pl.BlockSpec((D_MODEL, U_TILE), lambda i, j: (0, j)),
            pl.BlockSpec((D_MODEL, DT_PAD), lambda i, j: (0, 0)),
            pl.BlockSpec((1, U_TILE), gb_map),
            pl.BlockSpec((tm, RET_DK // 2), lambda i, j: (i % npos, 0)),
            pl.BlockSpec((tm, RET_DK // 2), lambda i, j: (i % npos, 0)),
        ],
        out_specs=[
            pl.BlockSpec((tm, U_TILE), lambda i, j: (i, j)),
            pl.BlockSpec((tm, DT_PAD), lambda i, j: (i, 0)),
        ],
        out_shape=[
            jax.ShapeDtypeStruct((n, U_COLS), BF16),
            jax.ShapeDtypeStruct((n, DT_PAD), F32),
        ],
        scratch_shapes=[pltpu.VMEM((tm, D_MODEL), BF16)],
        compiler_params=_cparams(("parallel", "arbitrary")),
        name="in_proj",
    )(x2d, prm["norm1_g"], prm["w_u"], prm["w_dt"], prm["gate_b"], cos, sin)


def _ssd_kernel(x_ref, b_ref, c_ref, z_ref, dt_ref,
                wx_ref, wb_ref, wc_ref, bx_ref, bb_ref, bc_ref,
                dtb_ref, alog_ref, dsk_ref, ng_ref,
                cx0_ref, cb0_ref, cc0_ref, s0_ref,
                y_ref, cxo_ref, cbo_ref, cco_ref, so_ref,
                s_scr, px_scr, pb_scr, pc_scr, *pad_scr, L, Tv):
    c = pl.program_id(1)
    nc = pl.num_programs(1)
    Q = LANES
    nq = L // Q

    @pl.when(c == 0)
    def _():
        s_scr[...] = s0_ref[0].reshape(SSD_D_INNER, SSD_STATE)
        px_scr[...] = cx0_ref[0]
        pb_scr[...] = cb0_ref[0]
        pc_scr[...] = cc0_ref[0]

    if Tv < L:
        for scr, ref in zip(pad_scr, (x_ref, b_ref, c_ref, z_ref, dt_ref)):
            scr[...] = jnp.zeros(scr.shape, scr.dtype)
            scr[0:Tv, :] = ref[...]
        x_in, b_in, c_in, z_in, dt_in = pad_scr
    else:
        x_in, b_in, c_in, z_in, dt_in = x_ref, b_ref, c_ref, z_ref, dt_ref

    rows = lax.broadcasted_iota(jnp.int32, (L, 1), 0)
    dtv = dt_in[...] + dtb_ref[...]
    dt = jnp.maximum(dtv, 0.0) + jnp.log(1.0 + jnp.exp(-jnp.abs(dtv)))
    if Tv < L:
        dt = jnp.where(rows < Tv, dt, 0.0)
    dA = dt * (-jnp.exp(alog_ref[...]))

    causal = (lax.broadcasted_iota(jnp.int32, (L, L), 0) >= lax.broadcasted_iota(jnp.int32, (L, L), 1))
    tri = jnp.where(causal, 1.0, 0.0).astype(BF16)
    hi, mid, lo = _split3(dA)
    cum = _dot(tri, hi) + _dot(tri, mid) + _dot(tri, lo)
    cum_last = cum[Tv - 1:Tv, :]
    ecum = jnp.exp(cum)
    todt = jnp.exp(cum_last - cum) * dt
    dec_tot = jnp.exp(cum_last)
    c2 = cum * LOG2E
    r_t = (c2 - jnp.log2(dt)).T
    tri_q = (lax.broadcasted_iota(jnp.int32, (Q, Q), 0) >= lax.broadcasted_iota(jnp.int32, (Q, Q), 1))

    lane = lax.broadcasted_iota(jnp.int32, (1, 2 * SSD_HEAD_DIM), 1)
    lo_half = lane < SSD_HEAD_DIM
    P2 = 2 * SSD_HEAD_DIM

    for g in range(SSD_GROUPS):
        gx = slice(g * SSD_GROUP_CH, (g + 1) * SSD_GROUP_CH)
        gn = slice(g * SSD_STATE, (g + 1) * SSD_STATE)
        xpre = x_in[:, gx].astype(F32)
        bpre = b_in[:, gn].astype(F32)
        cpre = c_in[:, gn].astype(F32)
        xc = _silu(_causal_conv(xpre, [px_scr[:, gx]], wx_ref[:, gx], bx_ref[:, gx]))
        bc = _silu(_causal_conv(bpre, [pb_scr[:, gn]], wb_ref[:, gn], bb_ref[:, gn]))
        cc = _silu(_causal_conv(cpre, [pc_scr[:, gn]], wc_ref[:, gn], bc_ref[:, gn]))
        px_scr[:, gx] = xpre[Tv - SUBLANES:Tv]
        pb_scr[:, gn] = bpre[Tv - SUBLANES:Tv]
        pc_scr[:, gn] = cpre[Tv - SUBLANES:Tv]

        xcb = xc.astype(BF16)
        bcb = bc.astype(BF16)
        ccb = cc.astype(BF16)
        cb = _dot_nt(ccb, bcb)
        s_old = s_scr[gx, :]
        y_inter = _dot_nt(ccb, s_old.astype(BF16))

        y_parts = []
        xw_parts = []
        for pr in range(HEADS_PER_GROUP // 2):
            heads = [g * HEADS_PER_GROUP + 2 * pr + k for k in range(2)]
            ps = slice(pr * P2, (pr + 1) * P2)
            x_pair = xc[:, ps]
            x_pair_b = xcb[:, ps]
            zero = jnp.zeros_like(x_pair_b)
            x_a = jnp.where(lo_half, x_pair_b, zero)
            x_b = jnp.where(lo_half, zero, x_pair_b)
            y_rows = []
            for i in range(nq):
                ri = slice(i * Q, (i + 1) * Q)
                lhs, rhs = [], []
                for h, xh in zip(heads, (x_a, x_b)):
                    for jq in range(i + 1):
                        rj = slice(jq * Q, (jq + 1) * Q)
                        e = jnp.exp2(c2[ri, h:h + 1] - r_t[h:h + 1, rj])
                        if jq == i:
                            e = jnp.where(tri_q, e, 0.0)
                        lhs.append((e * cb[ri, rj]).astype(BF16))
                    rhs.append(xh[0:(i + 1) * Q])
                y_rows.append(_dot(jnp.concatenate(lhs, axis=1), jnp.concatenate(rhs, axis=0)))
            y_pair = y_rows[0] if nq == 1 else jnp.concatenate(y_rows, axis=0)
            e_pair = jnp.where(lo_half, ecum[:, heads[0]:heads[0] + 1], ecum[:, heads[1]:heads[1] + 1])
            t_pair = jnp.where(lo_half, todt[:, heads[0]:heads[0] + 1], todt[:, heads[1]:heads[1] + 1])
            y_pair = y_pair + y_inter[:, ps] * e_pair
            y_pair = y_pair + x_pair * dsk_ref[:, g * SSD_GROUP_CH + pr * P2:g * SSD_GROUP_CH + (pr + 1) * P2]
            y_parts.append(y_pair)
            xw_parts.append((x_pair * t_pair).astype(BF16))

        y = jnp.concatenate(y_parts, axis=1)
        xw = jnp.concatenate(xw_parts, axis=1)

        dec_rows = [jnp.broadcast_to(dec_tot[:, g * HEADS_PER_GROUP + k:g * HEADS_PER_GROUP + k + 1],
                                     (SSD_HEAD_DIM, 1)) for k in range(HEADS_PER_GROUP)]
        s_new = s_old * jnp.concatenate(dec_rows, axis=0) + _dot_tn(xw, bcb)
        s_scr[gx, :] = s_new

        yz = y * z_in[:, gx].astype(F32)
        yn = yz * lax.rsqrt(jnp.mean(yz * yz, axis=-1, keepdims=True) + EPS) * ng_ref[:, gx]
        y_ref[:, gx] = yn[0:Tv].astype(BF16)

    @pl.when(c == nc - 1)
    def _():
        so_ref[0] = s_scr[...].reshape(SSD_HEADS, SSD_HEAD_DIM, SSD_STATE)
        cxo_ref[0] = px_scr[...]
        cbo_ref[0] = pb_scr[...]
        cco_ref[0] = pc_scr[...]


def _ssd(u, dt_raw, prm, cx0, cb0, cc0, s0, nb, t, L, Tv):
    nc = t // Tv
    rb = lambda b, c: b * nc + c
    full = lambda w: pl.BlockSpec(w.shape, lambda b, c: (0,) * w.ndim)
    per_b = lambda shp: pl.BlockSpec((1,) + shp, lambda b, c: (b,) + (0,) * len(shp))
    names = ["cw_x", "cw_b", "cw_c", "cb_x", "cb_b", "cb_c", "dt_bias", "a_log", "d_skip", "ssd_norm_g"]
    in_specs = [
        pl.BlockSpec((Tv, SSD_D_INNER), lambda b, c: (rb(b, c), COL_X // SSD_D_INNER)),
        pl.BlockSpec((Tv, SSD_BC), lambda b, c: (rb(b, c), COL_B // SSD_BC)),
        pl.BlockSpec((Tv, SSD_BC), lambda b, c: (rb(b, c), COL_C // SSD_BC)),
        pl.BlockSpec((Tv, SSD_D_INNER), lambda b, c: (rb(b, c), COL_Z // SSD_D_INNER)),
        pl.BlockSpec((Tv, DT_PAD), lambda b, c: (rb(b, c), 0)),
    ] + [full(prm[k]) for k in names] + [
        per_b((SUBLANES, SSD_D_INNER)), per_b((SUBLANES, SSD_BC)), per_b((SUBLANES, SSD_BC)),
        per_b((SSD_HEADS, SSD_HEAD_DIM, SSD_STATE)),
    ]
    out_specs = [
        pl.BlockSpec((Tv, SSD_D_INNER), lambda b, c: (rb(b, c), 0)),
        per_b((SUBLANES, SSD_D_INNER)), per_b((SUBLANES, SSD_BC)), per_b((SUBLANES, SSD_BC)),
        per_b((SSD_HEADS, SSD_HEAD_DIM, SSD_STATE)),
    ]
    out_shape = [
        jax.ShapeDtypeStruct((nb * t, SSD_D_INNER), BF16),
        jax.ShapeDtypeStruct((nb, SUBLANES, SSD_D_INNER), F32),
        jax.ShapeDtypeStruct((nb, SUBLANES, SSD_BC), F32),
        jax.ShapeDtypeStruct((nb, SUBLANES, SSD_BC), F32),
        jax.ShapeDtypeStruct((nb, SSD_HEADS, SSD_HEAD_DIM, SSD_STATE), F32),
    ]
    scratch = [
        pltpu.VMEM((SSD_D_INNER, SSD_STATE), F32),
        pltpu.VMEM((SUBLANES, SSD_D_INNER), F32),
        pltpu.VMEM((SUBLANES, SSD_BC), F32),
        pltpu.VMEM((SUBLANES, SSD_BC), F32),
    ]
    if Tv < L:
        scratch += [
            pltpu.VMEM((L, SSD_D_INNER), BF16), pltpu.VMEM((L, SSD_BC), BF16),
            pltpu.VMEM((L, SSD_BC), BF16), pltpu.VMEM((L, SSD_D_INNER), BF16),
            pltpu.VMEM((L, DT_PAD), F32),
        ]
    return pl.pallas_call(
        functools.partial(_ssd_kernel, L=L, Tv=Tv),
        grid=(nb, nc),
        in_specs=in_specs, out_specs=out_specs, out_shape=out_shape,
        scratch_shapes=scratch,
        compiler_params=_cparams(("parallel", "arbitrary")),
        name="ssd",
    )(u, u, u, u, dt_raw, *[prm[k] for k in names], cx0, cb0, cc0, s0)


def _ret_kernel(q_ref, k_ref, v_ref, g_ref, lg_ref, ng_ref, s0_ref,
                y_ref, so_ref, s_scr, dm_scr, cross_scr, kdec_scr, *pad_scr, L, Tv):
    c = pl.program_id(1)
    nc = pl.num_programs(1)

    @pl.when(c == 0)
    def _():
        s_scr[...] = s0_ref[0]
        rowf = lax.broadcasted_iota(jnp.int32, (L, 1), 0).astype(F32)
        colf = lax.broadcasted_iota(jnp.int32, (1, L), 1).astype(F32)
        diff = rowf - colf
        for h in range(RET_HEADS):
            lg = lg_ref[h][:, 0:1]
            dm_scr[h] = jnp.where(diff >= 0.0, jnp.exp(jnp.maximum(diff, 0.0) * lg), 0.0)
            cross_scr[h] = jnp.broadcast_to(jnp.exp((rowf + 1.0) * lg), (L, LANES))
            kdec = jnp.exp((Tv - 1.0 - rowf) * lg)
            if Tv < L:
                kdec = jnp.where(rowf < Tv, kdec, 0.0)
            kdec_scr[h] = jnp.broadcast_to(kdec, (L, LANES))

    if Tv < L:
        for scr, ref in zip(pad_scr, (q_ref, k_ref, v_ref, g_ref)):
            scr[...] = jnp.zeros(scr.shape, scr.dtype)
            scr[0:Tv, :] = ref[...]
        q_in, k_in, v_in, g_in = pad_scr
    else:
        q_in, k_in, v_in, g_in = q_ref, k_ref, v_ref, g_ref

    for h in range(RET_HEADS):
        ks = slice(h * RET_DK, (h + 1) * RET_DK)
        vs = slice(h * RET_DV, (h + 1) * RET_DV)
        qh = q_in[:, ks]
        kh = k_in[:, ks]
        vh = v_in[:, vs]
        s_old = s_scr[h]
        sc = (_dot_nt(qh, kh) * dm_scr[h]).astype(BF16)
        cross = jnp.concatenate([cross_scr[h]] * (RET_DV // LANES), axis=1)
        o = _dot(sc, vh) + _dot(qh, s_old.astype(BF16)) * cross
        kdec = jnp.concatenate([kdec_scr[h]] * (RET_DK // LANES), axis=1)
        kd = (kh.astype(F32) * kdec).astype(BF16)
        s_new = s_old * jnp.exp(Tv * lg_ref[h][:, 0:1]) + _dot_tn(kd, vh)
        s_scr[h] = s_new

        mu = jnp.mean(o, axis=-1, keepdims=True)
        d = o - mu
        var = jnp.mean(d * d, axis=-1, keepdims=True)
        y = d * lax.rsqrt(var + EPS) * ng_ref[:, vs] * g_in[:, vs].astype(F32)
        y_ref[:, vs] = y[0:Tv].astype(BF16)

    @pl.when(c == nc - 1)
    def _():
        so_ref[0] = s_scr[...]


def _ret(u, lg, ng, s0, nb, t, L, Tv):
    nc = t // Tv
    rb = lambda b, c: b * nc + c
    state_spec = pl.BlockSpec((1, RET_HEADS, RET_DK, RET_DV), lambda b, c: (b, 0, 0, 0))
    in_specs = [
        pl.BlockSpec((Tv, RET_QK), lambda b, c: (rb(b, c), COL_Q // RET_QK)),
        pl.BlockSpec((Tv, RET_QK), lambda b, c: (rb(b, c), COL_K // RET_QK)),
        pl.BlockSpec((Tv, RET_V), lambda b, c: (rb(b, c), COL_V // RET_V)),
        pl.BlockSpec((Tv, RET_V), lambda b, c: (rb(b, c), COL_G // RET_V)),
        pl.BlockSpec(lg.shape, lambda b, c: (0, 0, 0)),
        pl.BlockSpec((1, RET_V), lambda b, c: (0, 0)),
        state_spec,
    ]
    out_specs = [pl.BlockSpec((Tv, RET_V), lambda b, c: (rb(b, c), 0)), state_spec]
    out_shape = [
        jax.ShapeDtypeStruct((nb * t, RET_V), BF16),
        jax.ShapeDtypeStruct((nb, RET_HEADS, RET_DK, RET_DV), F32),
    ]
    scratch = [
        pltpu.VMEM((RET_HEADS, RET_DK, RET_DV), F32),
        pltpu.VMEM((RET_HEADS, L, L), F32),
        pltpu.VMEM((RET_HEADS, L, LANES), F32),
        pltpu.VMEM((RET_HEADS, L, LANES), F32),
    ]
    if Tv < L:
        scratch += [
            pltpu.VMEM((L, RET_QK), BF16), pltpu.VMEM((L, RET_QK), BF16),
            pltpu.VMEM((L, RET_V), BF16), pltpu.VMEM((L, RET_V), BF16),
        ]
    return pl.pallas_call(
        functools.partial(_ret_kernel, L=L, Tv=Tv),
        grid=(nb, nc),
        in_specs=in_specs, out_specs=out_specs, out_shape=out_shape,
        scratch_shapes=scratch,
        compiler_params=_cparams(("parallel", "arbitrary")),
        name="retention",
    )(u, u, u, u, lg, ng, s0)


def _merge_kernel(x_ref, ys_ref, yr_ref, gt_ref, ws_ref, wr_ref, wo_ref, o_ref):
    a = _dot(ys_ref[...], ws_ref[...])
    b = _dot(yr_ref[...], wr_ref[...])
    mix = gt_ref[:, :D_MODEL].astype(F32) * a + gt_ref[:, D_MODEL:].astype(F32) * b
    o_ref[...] = x_ref[...] + _dot(mix.astype(BF16), wo_ref[...])


def _merge(x2d, y_ssd, y_ret, u, w_s, w_r, w_o, tm):
    n = x2d.shape[0]
    return pl.pallas_call(
        _merge_kernel,
        grid=(n // tm,),
        in_specs=[
            pl.BlockSpec((tm, D_MODEL), lambda i: (i, 0)),
            pl.BlockSpec((tm, SSD_D_INNER), lambda i: (i, 0)),
            pl.BlockSpec((tm, RET_V), lambda i: (i, 0)),
            pl.BlockSpec((tm, 2 * D_MODEL), lambda i: (i, COL_GATES // (2 * D_MODEL))),
            _resident(w_s.shape), _resident(w_r.shape), _resident(w_o.shape),
        ],
        out_specs=pl.BlockSpec((tm, D_MODEL), lambda i: (i, 0)),
        out_shape=jax.ShapeDtypeStruct((n, D_MODEL), F32),
        compiler_params=_cparams(("parallel",)),
        name="merge",
    )(x2d, y_ssd, y_ret, u, w_s, w_r, w_o)


FFN_CHUNK = 256
FFN_NCHUNK = D_FF // FFN_CHUNK


def _ffn_kernel(x_ref, p_ref, n2_ref, wup_ref, cw_ref, cb_ref, wdn_ref, pg_ref, wpg_ref, wpp_ref,
                fg_ref, c0_ref, y_ref, co_ref, act_scr, carry_scr, *, tm, seg):
    nseg = tm // seg
    if nseg == 1:
        i = pl.program_id(1)

        @pl.when(i == 0)
        def _():
            carry_scr[...] = c0_ref[...]
        prev_ref = carry_scr
    else:
        prev_ref = c0_ref

    x1 = x_ref[...]
    h2 = _rms(x1, n2_ref[...]).astype(BF16)

    for cc in range(FFN_NCHUNK):
        halves = []
        for base in (0, D_FF):
            sl = slice(base + cc * FFN_CHUNK, base + (cc + 1) * FFN_CHUNK)
            up = _dot(h2, wup_ref[:, sl])
            prev8s = [prev_ref[s, :, sl] for s in range(nseg)]
            halves.append(_causal_conv(up, prev8s, cw_ref[:, sl], cb_ref[:, sl]))
            for s in range(nseg):
                tail = up[(s + 1) * seg - SUBLANES:(s + 1) * seg]
                if nseg == 1:
                    carry_scr[s, :, sl] = tail
                else:
                    co_ref[s, :, sl] = tail
        a, b = halves
        gelu = 0.5 * a * (1.0 + lax.erf(a * (2.0 ** -0.5)))
        act_scr[:, cc * FFN_CHUNK:(cc + 1) * FFN_CHUNK] = (gelu * b).astype(BF16)

    x2 = x1 + _dot(act_scr[...], wdn_ref[...])
    hg = _rms(x2, pg_ref[...]).astype(BF16)
    gate = _sigmoid(_dot(hg, wpg_ref[...]))
    x3 = x2 + gate * _dot(p_ref[...].astype(BF16), wpp_ref[...])
    y_ref[...] = _rms(x3, fg_ref[...])

    if nseg == 1:
        @pl.when(i == pl.num_programs(1) - 1)
        def _():
            co_ref[...] = carry_scr[...]


def _ffn(x1, p2d, prm, tails, nb, t, tm):
    n = nb * t
    seg = min(t, tm)
    nseg = tm // seg
    weights = [prm[k] for k in ("norm2_g", "w_up", "ffn_conv_w", "ffn_conv_b", "w_down", "ple_norm_g",
                                "w_ple_gate", "w_ple_proj", "final_norm_g")]
    wspecs = [_resident(w.shape) for w in weights]
    nt = t // seg
    grid = (n // (tm * nt), nt)
    row = lambda b, i: (b * nt + i, 0)
    tail_spec = pl.BlockSpec((nseg, SUBLANES, 2 * D_FF), lambda b, i: (b, 0, 0))
    return pl.pallas_call(
        functools.partial(_ffn_kernel, tm=tm, seg=seg),
        grid=grid,
        in_specs=[pl.BlockSpec((tm, D_MODEL), row), pl.BlockSpec((tm, D_PLE), row)] + wspecs + [tail_spec],
        out_specs=[pl.BlockSpec((tm, D_MODEL), row), tail_spec],
        out_shape=[jax.ShapeDtypeStruct((n, D_MODEL), F32),
                   jax.ShapeDtypeStruct((nb, SUBLANES, 2 * D_FF), F32)],
        scratch_shapes=[pltpu.VMEM((tm, D_FF), BF16), pltpu.VMEM((1, SUBLANES, 2 * D_FF), F32)],
        compiler_params=_cparams(("parallel", "arbitrary")),
        name="ffn",
    )(x1, p2d, *weights, tails)


def _prep_params(norm1_g, w_in, ssd_conv_w, ssd_conv_b, dt_bias, a_log, d_skip, ssd_norm_g, w_br_ssd,
                 ret_norm_g, w_br_ret, gate_b, w_out, norm2_g, w_up, ffn_conv_w, ffn_conv_b, w_down,
                 ple_norm_g, w_ple_gate, w_ple_proj, final_norm_g):
    o = {}
    sizes = (SSD_D_INNER, SSD_D_INNER, SSD_BC, SSD_BC, SSD_HEADS, RET_QK, RET_QK, RET_V, RET_V, 2 * D_MODEL)
    offs = [0]
    for s in sizes:
        offs.append(offs[-1] + s)
    wz, wx, wb, wc, wdt, wq, wk, wv, wg, wgt = [w_in[:, offs[i]:offs[i + 1]] for i in range(len(sizes))]
    o["w_u"] = jnp.concatenate([wz, wx, wv, wg, wgt, wq, wk, wb, wc], axis=1).astype(BF16)
    o["w_dt"] = jnp.pad(wdt, ((0, 0), (0, DT_PAD - SSD_HEADS))).astype(BF16)
    o["norm1_g"] = norm1_g.reshape(1, -1)
    o["cw_x"] = ssd_conv_w[:, :SSD_D_INNER]
    o["cw_b"] = ssd_conv_w[:, SSD_D_INNER:SSD_D_INNER + SSD_BC]
    o["cw_c"] = ssd_conv_w[:, SSD_D_INNER + SSD_BC:]
    cb = ssd_conv_b.reshape(1, -1)
    o["cb_x"] = cb[:, :SSD_D_INNER]
    o["cb_b"] = cb[:, SSD_D_INNER:SSD_D_INNER + SSD_BC]
    o["cb_c"] = cb[:, SSD_D_INNER + SSD_BC:]
    o["dt_bias"] = jnp.pad(dt_bias.reshape(1, -1), ((0, 0), (0, DT_PAD - SSD_HEADS)))
    o["a_log"] = jnp.pad(a_log.reshape(1, -1), ((0, 0), (0, DT_PAD - SSD_HEADS)))
    o["d_skip"] = jnp.repeat(d_skip, SSD_HEAD_DIM).reshape(1, -1)
    o["ssd_norm_g"] = ssd_norm_g.reshape(1, -1)
    o["w_br_ssd"] = w_br_ssd.astype(BF16)
    o["ret_norm_g"] = ret_norm_g.reshape(1, -1)
    o["w_br_ret"] = w_br_ret.astype(BF16)
    o["gate_b"] = gate_b.reshape(1, -1)
    o["w_out"] = w_out.astype(BF16)
    o["norm2_g"] = norm2_g.reshape(1, -1)
    o["w_up"] = w_up.astype(BF16)
    o["ffn_conv_w"] = ffn_conv_w
    o["ffn_conv_b"] = ffn_conv_b.reshape(1, -1)
    o["w_down"] = w_down.astype(BF16)
    o["ple_norm_g"] = ple_norm_g.reshape(1, -1)
    o["w_ple_gate"] = w_ple_gate.astype(BF16)
    o["w_ple_proj"] = w_ple_proj.astype(BF16)
    o["final_norm_g"] = final_norm_g.reshape(1, -1)
    lg = jnp.log1p(-jnp.power(2.0, -5.0 - jnp.arange(RET_HEADS, dtype=F32)))
    o["log_gamma"] = jnp.broadcast_to(lg[:, None, None], (RET_HEADS, 1, LANES))
    return o


def _pad_tail(buf):
    return jnp.pad(buf, ((0, 0), (SUBLANES - buf.shape[1], 0), (0, 0)))


def _tiles(nb, t):
    n = nb * t
    tm_in = min(2048, n)
    tm_merge = min(512, n)
    tm_ffn = min(512, t) if t >= 128 else min(128, n)
    tv = min(t, MIXER_CHUNK)
    return tm_in, tm_merge, tm_ffn, tv, max(tv, MIXER_MIN_ROWS)


def _group(x, p, pos0, conv_buf, ssd_s, ret_s, ffn_buf, prm):
    nb, t, _ = x.shape
    n = nb * t
    x2d = x.reshape(n, D_MODEL)
    p2d = p.reshape(n, D_PLE)
    tm_in, tm_merge, tm_ffn, Tv, L = _tiles(nb, t)

    half = RET_DK // 2
    inv = jnp.power(ROPE_BASE, -jnp.arange(half, dtype=F32) / half)
    ang = (pos0 + jnp.arange(t)).astype(F32)[:, None] * inv[None, :]
    cos, sin = jnp.cos(ang), jnp.sin(ang)
    if t < tm_in:
        cos, sin = jnp.tile(cos, (tm_in // t, 1)), jnp.tile(sin, (tm_in // t, 1))
    u, dt_raw = _in_proj(x2d, prm, cos, sin, tm_in)

    cx0 = _pad_tail(conv_buf[:, :, :SSD_D_INNER])
    cb0 = _pad_tail(conv_buf[:, :, SSD_D_INNER:SSD_D_INNER + SSD_BC])
    cc0 = _pad_tail(conv_buf[:, :, SSD_D_INNER + SSD_BC:])
    y_ssd, cxo, cbo, cco, ssd_new = _ssd(u, dt_raw, prm, cx0, cb0, cc0, ssd_s, nb, t, L, Tv)
    conv_new = jnp.concatenate([cxo, cbo, cco], axis=-1)[:, SUBLANES - (SSD_CONV - 1):, :]

    y_ret, ret_new = _ret(u, prm["log_gamma"], prm["ret_norm_g"], ret_s, nb, t, L, Tv)

    x1 = _merge(x2d, y_ssd, y_ret, u, prm["w_br_ssd"], prm["w_br_ret"], prm["w_out"], tm_merge)

    y, ffn_tail = _ffn(x1, p2d, prm, _pad_tail(ffn_buf), nb, t, tm_ffn)
    ffn_new = ffn_tail[:, SUBLANES - (FFN_CONV - 1):, :]
    return y.reshape(nb, t, D_MODEL), conv_new, ssd_new, ret_new, ffn_new


def kernel(x_prompt, x_sample, p_prompt, p_sample, state_ssd_conv, state_ssd, state_ret, state_ffn_conv,
           norm1_g, w_in, ssd_conv_w, ssd_conv_b, dt_bias, a_log, d_skip, ssd_norm_g, w_br_ssd,
           ret_norm_g, w_br_ret, gate_b, w_out, norm2_g, w_up, ffn_conv_w, ffn_conv_b, w_down,
           ple_norm_g, w_ple_gate, w_ple_proj, final_norm_g):
    assert norm1_g.shape[0] == 1, "single-layer model"
    prm = _prep_params(norm1_g[0], w_in[0], ssd_conv_w[0], ssd_conv_b[0], dt_bias[0], a_log[0], d_skip[0],
                       ssd_norm_g[0], w_br_ssd[0], ret_norm_g[0], w_br_ret[0], gate_b[0], w_out[0],
                       norm2_g[0], w_up[0], ffn_conv_w[0], ffn_conv_b[0], w_down[0], ple_norm_g[0],
                       w_ple_gate[0], w_ple_proj[0], final_norm_g)
    bp = x_prompt.shape[0]
    dtype = x_prompt.dtype
    zero_conv = jnp.zeros((bp, SSD_CONV - 1, SSD_D_INNER + 2 * SSD_BC), dtype)
    zero_ssd = jnp.zeros((bp, SSD_HEADS, SSD_HEAD_DIM, SSD_STATE), dtype)
    zero_ret = jnp.zeros((bp, RET_HEADS, RET_DK, RET_DV), dtype)
    zero_ffn = jnp.zeros((bp, FFN_CONV - 1, 2 * D_FF), dtype)
    yp, cp, sp, rp, fp = _group(x_prompt, p_prompt[0], 0, zero_conv, zero_ssd, zero_ret, zero_ffn, prm)
    ys, cs, ss, rs, fs = _group(x_sample, p_sample[0], PAST_LEN, state_ssd_conv[0], state_ssd[0],
                                state_ret[0], state_ffn_conv[0], prm)
    return (yp, ys, cp[None], sp[None], rp[None], fp[None], cs[None], ss[None], rs[None], fs[None])
```

```python
import functools

import jax
import jax.numpy as jnp
import numpy as np
from jax import lax
from jax.experimental import pallas as pl
from jax.experimental.pallas import tpu as pltpu

F32 = jnp.float32
BF16 = jnp.bfloat16

EPS = 1e-6
D_MODEL = 1024
D_PLE = 256
SSD_D_INNER = 2048
SSD_HEAD_DIM = 64
SSD_HEADS = 32
SSD_GROUPS = 4
SSD_STATE = 128
SSD_CONV = 4
SSD_BC = SSD_GROUPS * SSD_STATE
SSD_GROUP_CH = SSD_D_INNER // SSD_GROUPS
HEADS_PER_GROUP = SSD_HEADS // SSD_GROUPS
RET_HEADS = 4
RET_DK = 256
RET_DV = 512
RET_QK = RET_HEADS * RET_DK
RET_V = RET_HEADS * RET_DV
ROPE_BASE = 10000.0
D_FF = 2816
FFN_CONV = 3
PAST_LEN = 1024

U_TILE = 1024
U_COLS = 13 * U_TILE
COL_Z, COL_X, COL_V, COL_G, COL_GATES = 0, 2048, 4096, 6144, 8192
COL_Q, COL_K, COL_B, COL_C = 10240, 11264, 12288, 12800
TILE_Q, TILE_K, TILE_BC = COL_Q // U_TILE, COL_K // U_TILE, COL_B // U_TILE
DT_PAD = 128

LANES = 128
SUBLANES = 8
MIXER_CHUNK = 256
MIXER_MIN_ROWS = 128
VMEM_LIMIT = 56 * 1024 * 1024
LOG2E = 1.4426950408889634


def _cparams(sem):
    return pltpu.CompilerParams(dimension_semantics=sem, vmem_limit_bytes=VMEM_LIMIT)


def _resident(shape):
    nd = len(shape)
    return pl.BlockSpec(shape, lambda *_: (0,) * nd, pipeline_mode=pl.Buffered(1))


def _rms(x, g):
    return x * lax.rsqrt(jnp.mean(x * x, axis=-1, keepdims=True) + EPS) * g


def _sigmoid(x):
    return 1.0 / (1.0 + jnp.exp(-x))


def _silu(x):
    return x * _sigmoid(x)


def _split3(x):
    hi = x.astype(BF16)
    r1 = x - hi.astype(F32)
    mid = r1.astype(BF16)
    lo = (r1 - mid.astype(F32)).astype(BF16)
    return hi, mid, lo


def _dot(a, b):
    return jnp.dot(a, b, preferred_element_type=F32)


def _dot_nt(a, b):
    return lax.dot_general(a, b, (((1,), (1,)), ((), ())), preferred_element_type=F32)


def _dot_tn(a, b):
    return lax.dot_general(a, b, (((0,), (0,)), ((), ())), preferred_element_type=F32)


def _causal_conv(pre, prev8s, w, b):
    width = w.shape[0]
    rows, ch = pre.shape
    tiles_per_seg = rows // len(prev8s) // SUBLANES
    x3 = pre.reshape(rows // SUBLANES, SUBLANES, ch)
    sub = lax.broadcasted_iota(jnp.int32, (1, SUBLANES, 1), 1)
    acc = b.reshape(1, 1, ch) + w[width - 1:width].reshape(1, 1, ch) * x3
    for j in range(1, width):
        rot = pltpu.roll(x3, j, axis=1)
        pieces = []
        for s, prev8 in enumerate(prev8s):
            pieces.append(pltpu.roll(prev8.reshape(1, SUBLANES, ch), j, axis=1))
            if tiles_per_seg > 1:
                pieces.append(rot[s * tiles_per_seg:(s + 1) * tiles_per_seg - 1])
        prev = pieces[0] if len(pieces) == 1 else jnp.concatenate(pieces, axis=0)
        acc = acc + w[width - 1 - j:width - j].reshape(1, 1, ch) * jnp.where(sub < j, prev, rot)
    return acc.reshape(rows, ch)


INPROJ_SUB = 256


def _inproj_kernel(*refs, tm, seg, nt, has_c0):
    if has_c0:
        (x_ref, g_ref, w_ref, wdt_ref, gb_ref, cos_ref, sin_ref, cw_ref, cbias_ref, c0_ref,
         u_ref, dt_ref, co_ref, h_ref, carry_scr) = refs
    else:
        (x_ref, g_ref, w_ref, wdt_ref, gb_ref, cos_ref, sin_ref, cw_ref, cbias_ref,
         u_ref, dt_ref, co_ref, h_ref, carry_scr) = refs
        c0_ref = None
    i = pl.program_id(0)
    j = pl.program_id(1)
    sub = min(INPROJ_SUB, tm)
    nseg = tm // seg

    @pl.when(j == 0)
    def _():
        for r in range(tm // sub):
            rs = slice(r * sub, (r + 1) * sub)
            hb = _rms(x_ref[rs, :], g_ref[...]).astype(BF16)
            h_ref[rs, :] = hb
            dt_ref[rs, :] = _dot(hb, wdt_ref[...])

    def tiles(epilogue):
        for r in range(tm // sub):
            rs = slice(r * sub, (r + 1) * sub)
            u_ref[rs, :] = epilogue(_dot(h_ref[rs, :], w_ref[...]), rs).astype(BF16)

    def rotary(scale):
        def ep(acc, rs):
            cos = cos_ref[rs, :]
            sin = sin_ref[rs, :]
            half = RET_DK // 2
            out = []
            for h in range(U_TILE // RET_DK):
                x1 = acc[:, h * RET_DK:h * RET_DK + half]
                x2 = acc[:, h * RET_DK + half:(h + 1) * RET_DK]
                out += [(x1 * cos - x2 * sin) * scale, (x2 * cos + x1 * sin) * scale]
            return jnp.concatenate(out, axis=1)
        return ep

    is_silu = (j < COL_X // U_TILE) | ((j >= COL_G // U_TILE) & (j < COL_GATES // U_TILE))
    is_gate = (j >= COL_GATES // U_TILE) & (j < TILE_Q)
    is_conv = ((j >= COL_X // U_TILE) & (j < COL_V // U_TILE)) | (j == TILE_BC)
    is_plain = jnp.logical_not(is_silu | is_gate | is_conv | (j == TILE_Q) | (j == TILE_K))

    @pl.when(is_plain)
    def _():
        tiles(lambda acc, rs: acc)

    @pl.when(is_conv)
    def _():
        cblk = _conv_block(j)
        cw = cw_ref[...]
        cbias = cbias_ref[...]
        if nseg == 1:
            carried = carry_scr[cblk]
            init = c0_ref[0] if has_c0 else jnp.zeros_like(carried)
            prev = jnp.where(i % nt == 0, init, carried)
            for r in range(tm // sub):
                rs = slice(r * sub, (r + 1) * sub)
                acc = _dot(h_ref[rs, :], w_ref[...])
                u_ref[rs, :] = _silu(_causal_conv(acc, [prev], cw, cbias)).astype(BF16)
                prev = acc[sub - SUBLANES:sub]
            carry_scr[cblk] = prev
            co_ref[0] = prev
        else:
            acc = _dot(h_ref[...], w_ref[...])
            zero8 = jnp.zeros((SUBLANES, U_TILE), F32)
            prev8s = [c0_ref[s] if has_c0 else zero8 for s in range(nseg)]
            u_ref[...] = _silu(_causal_conv(acc, prev8s, cw, cbias)).astype(BF16)
            for s in range(nseg):
                co_ref[s] = acc[(s + 1) * seg - SUBLANES:(s + 1) * seg]

    @pl.when(is_silu)
    def _():
        tiles(lambda acc, rs: _silu(acc))

    @pl.when(is_gate)
    def _():
        tiles(lambda acc, rs: _sigmoid(acc + gb_ref[...]))

    @pl.when(j == TILE_Q)
    def _():
        tiles(rotary(1.0))

    @pl.when(j == TILE_K)
    def _():
        tiles(rotary(RET_DK ** -0.5))


def _conv_block(j):
    return jnp.where(j >= TILE_BC, 2, jnp.where(j > COL_X // U_TILE, 1, 0))


def _in_proj(x2d, prm, cos, sin, conv0, nb, t, tm):
    n = x2d.shape[0]
    npos = cos.shape[0] // tm
    seg = min(t, tm)
    nseg = tm // seg
    nt = t // seg
    assert nseg == 1 or tm <= INPROJ_SUB
    gate_tile0 = COL_GATES // U_TILE
    gb_map = lambda i, j: (0, jnp.clip(j - gate_tile0, 0, 2 * D_MODEL // U_TILE - 1))
    cmap = lambda i, j: (0, _conv_block(j))
    tail_spec = pl.BlockSpec((nseg, SUBLANES, U_TILE), lambda i, j: (i // nt, 0, _conv_block(j)))
    operands = [x2d, prm["norm1_g"], prm["w_u"], prm["w_dt"], prm["gate_b"], cos, sin,
                prm["ssd_conv_w"], prm["ssd_conv_b"]]
    in_specs = [
        pl.BlockSpec((tm, D_MODEL), lambda i, j: (i, 0)),
        pl.BlockSpec((1, D_MODEL), lambda i, j: (0, 0)),
        pl.BlockSpec((D_MODEL, U_TILE), lambda i, j: (0, j)),
        pl.BlockSpec((D_MODEL, DT_PAD), lambda i, j: (0, 0)),
        pl.BlockSpec((1, U_TILE), gb_map),
        pl.BlockSpec((tm, RET_DK // 2), lambda i, j: (i % npos, 0)),
        pl.BlockSpec((tm, RET_DK // 2), lambda i, j: (i % npos, 0)),
        pl.BlockSpec((SSD_CONV, U_TILE), cmap),
        pl.BlockSpec((1, U_TILE), cmap),
    ]
    if conv0 is not None:
        operands.append(conv0)
        in_specs.append(tail_spec)
    u, dt_raw, tails = pl.pallas_call(
        functools.partial(_inproj_kernel, tm=tm, seg=seg, nt=nt, has_c0=conv0 is not None),
        grid=(n // tm, U_COLS // U_TILE),
        in_specs=in_specs,
        out_specs=[
            pl.BlockSpec((tm, U_TILE), lambda i, j: (i, j)),
            pl.BlockSpec((tm, DT_PAD), lambda i, j: (i, 0)),
            pl.BlockSpec((nseg, SUBLANES, U_TILE), lambda i, j: (i, 0, _conv_block(j))),
        ],
        out_shape=[
            jax.ShapeDtypeStruct((n, U_COLS), BF16),
            jax.ShapeDtypeStruct((n, DT_PAD), F32),
            jax.ShapeDtypeStruct((nb * nt, SUBLANES, SSD_D_INNER + 2 * SSD_BC), F32),
        ],
        scratch_shapes=[pltpu.VMEM((tm, D_MODEL), BF16), pltpu.VMEM((3, SUBLANES, U_TILE), F32)],
        compiler_params=_cparams(("arbitrary", "arbitrary")),
        name="in_proj",
    )(*operands)
    xbc = SSD_D_INNER + 2 * SSD_BC
    return u, dt_raw, tails.reshape(nb, nt, SUBLANES, xbc)[:, nt - 1]


def _ssd_kernel(*refs, L, Tv, has_s0):
    if has_s0:
        (x_ref, b_ref, c_ref, z_ref, dt_ref, dtb_ref, alog_ref, dsk_ref, ng_ref, e128_ref, e64_ref, s0_ref,
         y_ref, so_ref, s_scr, *pad_scr) = refs
    else:
        (x_ref, b_ref, c_ref, z_ref, dt_ref, dtb_ref, alog_ref, dsk_ref, ng_ref, e128_ref, e64_ref,
         y_ref, so_ref, s_scr, *pad_scr) = refs
    c = pl.program_id(1)
    nc = pl.num_programs(1)
    Q = LANES
    nq = L // Q

    @pl.when(c == 0)
    def _():
        if has_s0:
            s_scr[...] = s0_ref[0].reshape(SSD_D_INNER, SSD_STATE)
        else:
            s_scr[...] = jnp.zeros(s_scr.shape, F32)

    if Tv < L:
        for scr, ref in zip(pad_scr, (x_ref, b_ref, c_ref, z_ref, dt_ref)):
            scr[...] = jnp.zeros(scr.shape, scr.dtype)
            scr[0:Tv, :] = ref[...]
        x_in, b_in, c_in, z_in, dt_in = pad_scr
    else:
        x_in, b_in, c_in, z_in, dt_in = x_ref, b_ref, c_ref, z_ref, dt_ref

    rows = lax.broadcasted_iota(jnp.int32, (L, 1), 0)
    dtv = dt_in[...] + dtb_ref[...]
    dt = jnp.maximum(dtv, 0.0) + jnp.log(1.0 + jnp.exp(-jnp.abs(dtv)))
    if Tv < L:
        dt = jnp.where(rows < Tv, dt, 0.0)
    dA = dt * (-jnp.exp(alog_ref[...]))

    causal = (lax.broadcasted_iota(jnp.int32, (L, L), 0) >= lax.broadcasted_iota(jnp.int32, (L, L), 1))
    tri = jnp.where(causal, 1.0, 0.0).astype(BF16)
    hi, mid, lo = _split3(dA)
    cum = _dot(tri, hi) + _dot(tri, mid) + _dot(tri, lo)
    dec_tot = jnp.exp(cum[Tv - 1:Tv, :])
    c2 = cum * LOG2E
    r_t = (c2 - jnp.log2(dt)).T
    tri_q = (lax.broadcasted_iota(jnp.int32, (Q, Q), 0) >= lax.broadcasted_iota(jnp.int32, (Q, Q), 1))

    lane = lax.broadcasted_iota(jnp.int32, (1, LANES), 1)

    def pack3(v):
        r1 = v - v.astype(BF16).astype(F32)
        r2 = r1 - r1.astype(BF16).astype(F32)
        return jnp.where(lane < SSD_HEADS, v,
                         jnp.where(lane < 2 * SSD_HEADS, pltpu.roll(r1, SSD_HEADS, axis=1),
                                   pltpu.roll(r2, 2 * SSD_HEADS, axis=1))).astype(BF16)

    c2p = pack3(c2)
    c2_x128 = _dot(c2p, e128_ref[...])
    c2_x64 = _dot(c2p, e64_ref[...])
    dt_x64 = _dot(pack3(dt), e64_ref[...])
    ecum_x = jnp.exp2(c2_x64)
    todt_x = jnp.exp2(c2_x64[Tv - 1:Tv, :] - c2_x64) * dt_x64

    lo_half = lane < SSD_HEAD_DIM
    P2 = 2 * SSD_HEAD_DIM

    for g in range(SSD_GROUPS):
        gx = slice(g * SSD_GROUP_CH, (g + 1) * SSD_GROUP_CH)
        gn = slice(g * SSD_STATE, (g + 1) * SSD_STATE)
        xcb = x_in[:, gx]
        bcb = b_in[:, gn]
        ccb = c_in[:, gn]
        xc = xcb.astype(F32)
        cb = _dot_nt(ccb, bcb)
        s_old = s_scr[gx, :]
        y_inter = _dot_nt(ccb, s_old.astype(BF16))

        y_parts = []
        xw_parts = []
        for pr in range(HEADS_PER_GROUP // 2):
            heads = [g * HEADS_PER_GROUP + 2 * pr + k for k in range(2)]
            ps = slice(pr * P2, (pr + 1) * P2)
            x_pair = xc[:, ps]
            x_pair_b = xcb[:, ps]
            zero = jnp.zeros_like(x_pair_b)
            x_a = jnp.where(lo_half, x_pair_b, zero)
            x_b = jnp.where(lo_half, zero, x_pair_b)
            y_rows = []
            for i in range(nq):
                ri = slice(i * Q, (i + 1) * Q)
                lhs, rhs = [], []
                for h, xh in zip(heads, (x_a, x_b)):
                    for jq in range(i + 1):
                        rj = slice(jq * Q, (jq + 1) * Q)
                        e = jnp.exp2(c2_x128[ri, h * LANES:(h + 1) * LANES] - r_t[h:h + 1, rj])
                        if jq == i:
                            e = jnp.where(tri_q, e, 0.0)
                        lhs.append((e * cb[ri, rj]).astype(BF16))
                    rhs.append(xh[0:(i + 1) * Q])
                y_rows.append(_dot(jnp.concatenate(lhs, axis=1), jnp.concatenate(rhs, axis=0)))
            y_pair = y_rows[0] if nq == 1 else jnp.concatenate(y_rows, axis=0)
            pg = slice(g * SSD_GROUP_CH + pr * P2, g * SSD_GROUP_CH + (pr + 1) * P2)
            y_pair = y_pair + y_inter[:, ps] * ecum_x[:, pg]
            y_pair = y_pair + x_pair * dsk_ref[:, pg]
            y_parts.append(y_pair)
            xw_parts.append((x_pair * todt_x[:, pg]).astype(BF16))

        y = jnp.concatenate(y_parts, axis=1)
        xw = jnp.concatenate(xw_parts, axis=1)

        dec_rows = [jnp.broadcast_to(dec_tot[:, g * HEADS_PER_GROUP + k:g * HEADS_PER_GROUP + k + 1],
                                     (SSD_HEAD_DIM, 1)) for k in range(HEADS_PER_GROUP)]
        s_new = s_old * jnp.concatenate(dec_rows, axis=0) + _dot_tn(xw, bcb)
        s_scr[gx, :] = s_new

        yz = y * z_in[:, gx].astype(F32)
        yn = yz * lax.rsqrt(jnp.mean(yz * yz, axis=-1, keepdims=True) + EPS) * ng_ref[:, gx]
        y_ref[:, gx] = yn[0:Tv].astype(BF16)

    @pl.when(c == nc - 1)
    def _():
        so_ref[0] = s_scr[...].reshape(SSD_HEADS, SSD_HEAD_DIM, SSD_STATE)


def _ssd(u, dt_raw, prm, s0, nb, t, L, Tv):
    nc = t // Tv
    rb = lambda b, c: b * nc + c
    full = lambda w: pl.BlockSpec(w.shape, lambda b, c: (0,) * w.ndim)
    state_spec = pl.BlockSpec((1, SSD_HEADS, SSD_HEAD_DIM, SSD_STATE), lambda b, c: (b, 0, 0, 0))
    names = ["dt_bias", "a_log", "d_skip", "ssd_norm_g", "expand128", "expand64"]
    operands = [u, u, u, u, dt_raw] + [prm[k] for k in names]
    in_specs = [
        pl.BlockSpec((Tv, SSD_D_INNER), lambda b, c: (rb(b, c), COL_X // SSD_D_INNER)),
        pl.BlockSpec((Tv, SSD_BC), lambda b, c: (rb(b, c), COL_B // SSD_BC)),
        pl.BlockSpec((Tv, SSD_BC), lambda b, c: (rb(b, c), COL_C // SSD_BC)),
        pl.BlockSpec((Tv, SSD_D_INNER), lambda b, c: (rb(b, c), COL_Z // SSD_D_INNER)),
        pl.BlockSpec((Tv, DT_PAD), lambda b, c: (rb(b, c), 0)),
    ] + [full(prm[k]) for k in names]
    if s0 is not None:
        operands.append(s0)
        in_specs.append(state_spec)
    scratch = [pltpu.VMEM((SSD_D_INNER, SSD_STATE), F32)]
    if Tv < L:
        scratch += [
            pltpu.VMEM((L, SSD_D_INNER), BF16), pltpu.VMEM((L, SSD_BC), BF16),
            pltpu.VMEM((L, SSD_BC), BF16), pltpu.VMEM((L, SSD_D_INNER), BF16),
            pltpu.VMEM((L, DT_PAD), F32),
        ]
    return pl.pallas_call(
        functools.partial(_ssd_kernel, L=L, Tv=Tv, has_s0=s0 is not None),
        grid=(nb, nc),
        in_specs=in_specs,
        out_specs=[pl.BlockSpec((Tv, SSD_D_INNER), lambda b, c: (rb(b, c), 0)), state_spec],
        out_shape=[jax.ShapeDtypeStruct((nb * t, SSD_D_INNER), BF16),
                   jax.ShapeDtypeStruct((nb, SSD_HEADS, SSD_HEAD_DIM, SSD_STATE), F32)],
        scratch_shapes=scratch,
        compiler_params=_cparams(("parallel", "arbitrary")),
        name="ssd",
    )(*operands)


def _ret_kernel(*refs, L, Tv, has_s0):
    if has_s0:
        (q_ref, k_ref, v_ref, g_ref, lg_ref, ng_ref, s0_ref,
         y_ref, so_ref, s_scr, dm_scr, cross_scr, kdec_scr, *pad_scr) = refs
    else:
        (q_ref, k_ref, v_ref, g_ref, lg_ref, ng_ref,
         y_ref, so_ref, s_scr, dm_scr, cross_scr, kdec_scr, *pad_scr) = refs
    c = pl.program_id(1)
    nc = pl.num_programs(1)

    @pl.when(c == 0)
    def _():
        if has_s0:
            s_scr[...] = s0_ref[0]
        else:
            s_scr[...] = jnp.zeros(s_scr.shape, F32)
        rowf = lax.broadcasted_iota(jnp.int32, (L, 1), 0).astype(F32)
        colf = lax.broadcasted_iota(jnp.int32, (1, L), 1).astype(F32)
        diff = rowf - colf
        for h in range(RET_HEADS):
            lg = lg_ref[h][:, 0:1]
            dm_scr[h] = jnp.where(diff >= 0.0, jnp.exp(jnp.maximum(diff, 0.0) * lg), 0.0)
            cross_scr[h] = jnp.broadcast_to(jnp.exp((rowf + 1.0) * lg), (L, LANES))
            kdec = jnp.exp((Tv - 1.0 - rowf) * lg)
            if Tv < L:
                kdec = jnp.where(rowf < Tv, kdec, 0.0)
            kdec_scr[h] = jnp.broadcast_to(kdec, (L, LANES))

    if Tv < L:
        for scr, ref in zip(pad_scr, (q_ref, k_ref, v_ref, g_ref)):
            scr[...] = jnp.zeros(scr.shape, scr.dtype)
            scr[0:Tv, :] = ref[...]
        q_in, k_in, v_in, g_in = pad_scr
    else:
        q_in, k_in, v_in, g_in = q_ref, k_ref, v_ref, g_ref

    for h in range(RET_HEADS):
        ks = slice(h * RET_DK, (h + 1) * RET_DK)
        vs = slice(h * RET_DV, (h + 1) * RET_DV)
        qh = q_in[:, ks]
        kh = k_in[:, ks]
        vh = v_in[:, vs]
        s_old = s_scr[h]
        sc = (_dot_nt(qh, kh) * dm_scr[h]).astype(BF16)
        cross = jnp.concatenate([cross_scr[h]] * (RET_DV // LANES), axis=1)
        o = _dot(sc, vh) + _dot(qh, s_old.astype(BF16)) * cross
        kdec = jnp.concatenate([kdec_scr[h]] * (RET_DK // LANES), axis=1)
        kd = (kh.astype(F32) * kdec).astype(BF16)
        s_new = s_old * jnp.exp(Tv * lg_ref[h][:, 0:1]) + _dot_tn(kd, vh)
        s_scr[h] = s_new

        mu = jnp.mean(o, axis=-1, keepdims=True)
        d = o - mu
        var = jnp.mean(d * d, axis=-1, keepdims=True)
        y = d * lax.rsqrt(var + EPS) * ng_ref[:, vs] * g_in[:, vs].astype(F32)
        y_ref[:, vs] = y[0:Tv].astype(BF16)

    @pl.when(c == nc - 1)
    def _():
        so_ref[0] = s_scr[...]


def _ret(u, lg, ng, s0, nb, t, L, Tv):
    nc = t // Tv
    rb = lambda b, c: b * nc + c
    state_spec = pl.BlockSpec((1, RET_HEADS, RET_DK, RET_DV), lambda b, c: (b, 0, 0, 0))
    in_specs = [
        pl.BlockSpec((Tv, RET_QK), lambda b, c: (rb(b, c), COL_Q // RET_QK)),
        pl.BlockSpec((Tv, RET_QK), lambda b, c: (rb(b, c), COL_K // RET_QK)),
        pl.BlockSpec((Tv, RET_V), lambda b, c: (rb(b, c), COL_V // RET_V)),
        pl.BlockSpec((Tv, RET_V), lambda b, c: (rb(b, c), COL_G // RET_V)),
        pl.BlockSpec(lg.shape, lambda b, c: (0, 0, 0)),
        pl.BlockSpec((1, RET_V), lambda b, c: (0, 0)),
    ]
    operands = [u, u, u, u, lg, ng]
    if s0 is not None:
        operands.append(s0)
        in_specs.append(state_spec)
    out_specs = [pl.BlockSpec((Tv, RET_V), lambda b, c: (rb(b, c), 0)), state_spec]
    out_shape = [
        jax.ShapeDtypeStruct((nb * t, RET_V), BF16),
        jax.ShapeDtypeStruct((nb, RET_HEADS, RET_DK, RET_DV), F32),
    ]
    scratch = [
        pltpu.VMEM((RET_HEADS, RET_DK, RET_DV), F32),
        pltpu.VMEM((RET_HEADS, L, L), F32),
        pltpu.VMEM((RET_HEADS, L, LANES), F32),
        pltpu.VMEM((RET_HEADS, L, LANES), F32),
    ]
    if Tv < L:
        scratch += [
            pltpu.VMEM((L, RET_QK), BF16), pltpu.VMEM((L, RET_QK), BF16),
            pltpu.VMEM((L, RET_V), BF16), pltpu.VMEM((L, RET_V), BF16),
        ]
    return pl.pallas_call(
        functools.partial(_ret_kernel, L=L, Tv=Tv, has_s0=s0 is not None),
        grid=(nb, nc),
        in_specs=in_specs, out_specs=out_specs, out_shape=out_shape,
        scratch_shapes=scratch,
        compiler_params=_cparams(("parallel", "arbitrary")),
        name="retention",
    )(*operands)


def _merge_kernel(x_ref, ys_ref, yr_ref, gt_ref, ws_ref, wr_ref, wo_ref, o_ref):
    a = _dot(ys_ref[...], ws_ref[...])
    b = _dot(yr_ref[...], wr_ref[...])
    mix = gt_ref[:, :D_MODEL].astype(F32) * a + gt_ref[:, D_MODEL:].astype(F32) * b
    o_ref[...] = x_ref[...] + _dot(mix.astype(BF16), wo_ref[...])


def _merge(x2d, y_ssd, y_ret, u, w_s, w_r, w_o, tm):
    n = x2d.shape[0]
    return pl.pallas_call(
        _merge_kernel,
        grid=(n // tm,),
        in_specs=[
            pl.BlockSpec((tm, D_MODEL), lambda i: (i, 0)),
            pl.BlockSpec((tm, SSD_D_INNER), lambda i: (i, 0)),
            pl.BlockSpec((tm, RET_V), lambda i: (i, 0)),
            pl.BlockSpec((tm, 2 * D_MODEL), lambda i: (i, COL_GATES // (2 * D_MODEL))),
            _resident(w_s.shape), _resident(w_r.shape), _resident(w_o.shape),
        ],
        out_specs=pl.BlockSpec((tm, D_MODEL), lambda i: (i, 0)),
        out_shape=jax.ShapeDtypeStruct((n, D_MODEL), F32),
        compiler_params=_cparams(("parallel",)),
        name="merge",
    )(x2d, y_ssd, y_ret, u, w_s, w_r, w_o)


FFN_CHUNK = 256
FFN_NCHUNK = D_FF // FFN_CHUNK


def _ffn_kernel(*refs, tm, seg, has_c0):
    if has_c0:
        (x_ref, p_ref, n2_ref, wup_ref, cw_ref, cb_ref, wdn_ref, pg_ref, wpg_ref, wpp_ref, fg_ref, c0_ref,
         y_ref, co_ref, act_scr, carry_scr) = refs
    else:
        (x_ref, p_ref, n2_ref, wup_ref, cw_ref, cb_ref, wdn_ref, pg_ref, wpg_ref, wpp_ref, fg_ref,
         y_ref, co_ref, act_scr, carry_scr) = refs
    nseg = tm // seg
    if nseg == 1:
        i = pl.program_id(1)

        @pl.when(i == 0)
        def _():
            if has_c0:
                carry_scr[...] = c0_ref[...]
            else:
                carry_scr[...] = jnp.zeros(carry_scr.shape, F32)
        prev_ref = carry_scr
    else:
        assert has_c0
        prev_ref = c0_ref

    x1 = x_ref[...]
    h2 = _rms(x1, n2_ref[...]).astype(BF16)

    for cc in range(FFN_NCHUNK):
        halves = []
        for base in (0, D_FF):
            sl = slice(base + cc * FFN_CHUNK, base + (cc + 1) * FFN_CHUNK)
            up = _dot(h2, wup_ref[:, sl])
            prev8s = [prev_ref[s, :, sl] for s in range(nseg)]
            halves.append(_causal_conv(up, prev8s, cw_ref[:, sl], cb_ref[:, sl]))
            for s in range(nseg):
                tail = up[(s + 1) * seg - SUBLANES:(s + 1) * seg]
                if nseg == 1:
                    carry_scr[s, :, sl] = tail
                else:
                    co_ref[s, :, sl] = tail
        a, b = halves
        gelu = 0.5 * a * (1.0 + lax.erf(a * (2.0 ** -0.5)))
        act_scr[:, cc * FFN_CHUNK:(cc + 1) * FFN_CHUNK] = (gelu * b).astype(BF16)

    x2 = x1 + _dot(act_scr[...], wdn_ref[...])
    hg = _rms(x2, pg_ref[...]).astype(BF16)
    gate = _sigmoid(_dot(hg, wpg_ref[...]))
    x3 = x2 + gate * _dot(p_ref[...].astype(BF16), wpp_ref[...])
    y_ref[...] = _rms(x3, fg_ref[...])

    if nseg == 1:
        @pl.when(i == pl.num_programs(1) - 1)
        def _():
            co_ref[...] = carry_scr[...]


def _ffn(x1, p2d, prm, tails, nb, t, tm):
    n = nb * t
    seg = min(t, tm)
    nseg = tm // seg
    weights = [prm[k] for k in ("norm2_g", "w_up", "ffn_conv_w", "ffn_conv_b", "w_down", "ple_norm_g",
                                "w_ple_gate", "w_ple_proj", "final_norm_g")]
    wspecs = [_resident(w.shape) for w in weights]
    nt = t // seg
    grid = (n // (tm * nt), nt)
    row = lambda b, i: (b * nt + i, 0)
    tail_spec = pl.BlockSpec((nseg, SUBLANES, 2 * D_FF), lambda b, i: (b, 0, 0))
    operands = [x1, p2d, *weights]
    in_specs = [pl.BlockSpec((tm, D_MODEL), row), pl.BlockSpec((tm, D_PLE), row)] + wspecs
    if tails is not None:
        operands.append(tails)
        in_specs.append(tail_spec)
    return pl.pallas_call(
        functools.partial(_ffn_kernel, tm=tm, seg=seg, has_c0=tails is not None),
        grid=grid,
        in_specs=in_specs,
        out_specs=[pl.BlockSpec((tm, D_MODEL), row), tail_spec],
        out_shape=[jax.ShapeDtypeStruct((n, D_MODEL), F32),
                   jax.ShapeDtypeStruct((nb, SUBLANES, 2 * D_FF), F32)],
        scratch_shapes=[pltpu.VMEM((tm, D_FF), BF16), pltpu.VMEM((1, SUBLANES, 2 * D_FF), F32)],
        compiler_params=_cparams(("parallel", "arbitrary")),
        name="ffn",
    )(*operands)


def _prep_params(norm1_g, w_in, ssd_conv_w, ssd_conv_b, dt_bias, a_log, d_skip, ssd_norm_g, w_br_ssd,
                 ret_norm_g, w_br_ret, gate_b, w_out, norm2_g, w_up, ffn_conv_w, ffn_conv_b, w_down,
                 ple_norm_g, w_ple_gate, w_ple_proj, final_norm_g):
    o = {}
    sizes = (SSD_D_INNER, SSD_D_INNER, SSD_BC, SSD_BC, SSD_HEADS, RET_QK, RET_QK, RET_V, RET_V, 2 * D_MODEL)
    offs = [0]
    for s in sizes:
        offs.append(offs[-1] + s)
    wz, wx, wb, wc, wdt, wq, wk, wv, wg, wgt = [w_in[:, offs[i]:offs[i + 1]] for i in range(len(sizes))]
    o["w_u"] = jnp.concatenate([wz, wx, wv, wg, wgt, wq, wk, wb, wc], axis=1).astype(BF16)
    o["w_dt"] = jnp.pad(wdt, ((0, 0), (0, DT_PAD - SSD_HEADS))).astype(BF16)
    o["norm1_g"] = norm1_g.reshape(1, -1)
    o["ssd_conv_w"] = ssd_conv_w
    o["ssd_conv_b"] = ssd_conv_b.reshape(1, -1)
    o["dt_bias"] = jnp.pad(dt_bias.reshape(1, -1), ((0, 0), (0, DT_PAD - SSD_HEADS)))
    o["a_log"] = jnp.pad(a_log.reshape(1, -1), ((0, 0), (0, DT_PAD - SSD_HEADS)))
    o["d_skip"] = jnp.repeat(d_skip, SSD_HEAD_DIM).reshape(1, -1)
    o["ssd_norm_g"] = ssd_norm_g.reshape(1, -1)
    o["w_br_ssd"] = w_br_ssd.astype(BF16)
    o["ret_norm_g"] = ret_norm_g.reshape(1, -1)
    o["w_br_ret"] = w_br_ret.astype(BF16)
    o["gate_b"] = gate_b.reshape(1, -1)
    o["w_out"] = w_out.astype(BF16)
    o["norm2_g"] = norm2_g.reshape(1, -1)
    o["w_up"] = w_up.astype(BF16)
    o["ffn_conv_w"] = ffn_conv_w
    o["ffn_conv_b"] = ffn_conv_b.reshape(1, -1)
    o["w_down"] = w_down.astype(BF16)
    o["ple_norm_g"] = ple_norm_g.reshape(1, -1)
    o["w_ple_gate"] = w_ple_gate.astype(BF16)
    o["w_ple_proj"] = w_ple_proj.astype(BF16)
    o["final_norm_g"] = final_norm_g.reshape(1, -1)
    piece_head = np.arange(LANES) % SSD_HEADS
    used = np.arange(LANES) < 3 * SSD_HEADS
    for width in (LANES, SSD_HEAD_DIM):
        col_head = np.arange(SSD_HEADS * width) // width
        e = (piece_head[:, None] == col_head[None, :]) & used[:, None]
        o["expand%d" % width] = jnp.asarray(e, BF16)
    lg = np.log1p(-np.power(2.0, -5.0 - np.arange(RET_HEADS)))
    o["log_gamma"] = jnp.asarray(np.broadcast_to(lg[:, None, None], (RET_HEADS, 1, LANES)), F32)
    return o


def _rope_tables(pos0, t):
    half = RET_DK // 2
    inv = np.power(ROPE_BASE, -np.arange(half) / half)
    ang = (pos0 + np.arange(t))[:, None] * inv[None, :]
    return np.cos(ang).astype(np.float32), np.sin(ang).astype(np.float32)


def _pad_tail(buf):
    return jnp.pad(buf, ((0, 0), (SUBLANES - buf.shape[1], 0), (0, 0)))


def _tiles(nb, t):
    n = nb * t
    tm_in = min(2048, t) if t >= INPROJ_SUB else min(INPROJ_SUB, n)
    tm_merge = min(512, n)
    tm_ffn = min(512, t) if t >= 128 else min(128, n)
    tv = min(t, MIXER_CHUNK)
    return tm_in, tm_merge, tm_ffn, tv, max(tv, MIXER_MIN_ROWS)


def _group(x, p, pos0, states, prm):
    nb, t, _ = x.shape
    n = nb * t
    x2d = x.reshape(n, D_MODEL)
    p2d = p.reshape(n, D_PLE)
    tm_in, tm_merge, tm_ffn, Tv, L = _tiles(nb, t)
    if states is None:
        conv0 = ssd_s = ret_s = ffn0 = None
    else:
        conv_buf, ssd_s, ret_s, ffn_buf = states
        conv0, ffn0 = _pad_tail(conv_buf), _pad_tail(ffn_buf)

    cos, sin = _rope_tables(pos0, t)
    if t < tm_in:
        cos, sin = np.tile(cos, (tm_in // t, 1)), np.tile(sin, (tm_in // t, 1))
    u, dt_raw, conv_tail = _in_proj(x2d, prm, jnp.asarray(cos), jnp.asarray(sin), conv0, nb, t, tm_in)
    conv_new = conv_tail[:, SUBLANES - (SSD_CONV - 1):, :]

    y_ssd, ssd_new = _ssd(u, dt_raw, prm, ssd_s, nb, t, L, Tv)
    y_ret, ret_new = _ret(u, prm["log_gamma"], prm["ret_norm_g"], ret_s, nb, t, L, Tv)
    x1 = _merge(x2d, y_ssd, y_ret, u, prm["w_br_ssd"], prm["w_br_ret"], prm["w_out"], tm_merge)
    y, ffn_tail = _ffn(x1, p2d, prm, ffn0, nb, t, tm_ffn)
    ffn_new = ffn_tail[:, SUBLANES - (FFN_CONV - 1):, :]
    return y.reshape(nb, t, D_MODEL), conv_new, ssd_new, ret_new, ffn_new


def kernel(x_prompt, x_sample, p_prompt, p_sample, state_ssd_conv, state_ssd, state_ret, state_ffn_conv,
           norm1_g, w_in, ssd_conv_w, ssd_conv_b, dt_bias, a_log, d_skip, ssd_norm_g, w_br_ssd,
           ret_norm_g, w_br_ret, gate_b, w_out, norm2_g, w_up, ffn_conv_w, ffn_conv_b, w_down,
           ple_norm_g, w_ple_gate, w_ple_proj, final_norm_g):
    assert norm1_g.shape[0] == 1, "single-layer model"
    prm = _prep_params(norm1_g[0], w_in[0], ssd_conv_w[0], ssd_conv_b[0], dt_bias[0], a_log[0], d_skip[0],
                       ssd_norm_g[0], w_br_ssd[0], ret_norm_g[0], w_br_ret[0], gate_b[0], w_out[0],
                       norm2_g[0], w_up[0], ffn_conv_w[0], ffn_conv_b[0], w_down[0], ple_norm_g[0],
                       w_ple_gate[0], w_ple_proj[0], final_norm_g)
    yp, cp, sp, rp, fp = _group(x_prompt, p_prompt[0], 0, None, prm)
    ys, cs, ss, rs, fs = _group(x_sample, p_sample[0], PAST_LEN,
                                (state_ssd_conv[0], state_ssd[0], state_ret[0], state_ffn_conv[0]), prm)
    return (yp, ys, cp[None], sp[None], rp[None], fp[None], cs[None], ss[None], rs[None], fs[None])
```

```python
import functools

import jax
import jax.numpy as jnp
import numpy as np
from jax import lax
from jax.experimental import pallas as pl
from jax.experimental.pallas import tpu as pltpu

F32 = jnp.float32
BF16 = jnp.bfloat16

EPS = 1e-6
D_MODEL = 1024
D_PLE = 256
SSD_D_INNER = 2048
SSD_HEAD_DIM = 64
SSD_HEADS = 32
SSD_GROUPS = 4
SSD_STATE = 128
SSD_CONV = 4
SSD_BC = SSD_GROUPS * SSD_STATE
SSD_GROUP_CH = SSD_D_INNER // SSD_GROUPS
HEADS_PER_GROUP = SSD_HEADS // SSD_GROUPS
RET_HEADS = 4
RET_DK = 256
RET_DV = 512
RET_QK = RET_HEADS * RET_DK
RET_V = RET_HEADS * RET_DV
ROPE_BASE = 10000.0
D_FF = 2816
FFN_CONV = 3
PAST_LEN = 1024

U_TILE = 1024
U_COLS = 13 * U_TILE
COL_Z, COL_X, COL_V, COL_G, COL_GATES = 0, 2048, 4096, 6144, 8192
COL_Q, COL_K, COL_B, COL_C = 10240, 11264, 12288, 12800
TILE_V, TILE_Q, TILE_K, TILE_BC = COL_V // U_TILE, COL_Q // U_TILE, COL_K // U_TILE, COL_B // U_TILE
DT_PAD = 128

LANES = 128
SUBLANES = 8
MIXER_CHUNK = 256
MIXER_MIN_ROWS = 128
VMEM_LIMIT = 56 * 1024 * 1024
LOG2E = 1.4426950408889634


def _cparams(sem):
    return pltpu.CompilerParams(dimension_semantics=sem, vmem_limit_bytes=VMEM_LIMIT)


def _resident(shape):
    nd = len(shape)
    return pl.BlockSpec(shape, lambda *_: (0,) * nd, pipeline_mode=pl.Buffered(1))


def _rms(x, g):
    return x * lax.rsqrt(jnp.mean(x * x, axis=-1, keepdims=True) + EPS) * g


def _sigmoid(x):
    return 0.5 + 0.5 * jnp.tanh(0.5 * x)


def _silu(x):
    h = 0.5 * x
    return h + h * jnp.tanh(h)


def _split3(x):
    hi = x.astype(BF16)
    r1 = x - hi.astype(F32)
    mid = r1.astype(BF16)
    lo = (r1 - mid.astype(F32)).astype(BF16)
    return hi, mid, lo


def _dot(a, b):
    return jnp.dot(a, b, preferred_element_type=F32)


def _dot_nt(a, b):
    return lax.dot_general(a, b, (((1,), (1,)), ((), ())), preferred_element_type=F32)


def _dot_tn(a, b):
    return lax.dot_general(a, b, (((0,), (0,)), ((), ())), preferred_element_type=F32)


def _causal_conv(pre, prev8s, w, b):
    width = w.shape[0]
    rows, ch = pre.shape
    tiles_per_seg = rows // len(prev8s) // SUBLANES
    sub = lax.broadcasted_iota(jnp.int32, (1, SUBLANES, 1), 1)
    strips = []
    for c0 in range(0, ch, LANES):
        cs = slice(c0, c0 + LANES)
        x3 = pre[:, cs].reshape(rows // SUBLANES, SUBLANES, LANES)
        acc = b[:, cs].reshape(1, 1, LANES) + w[width - 1:width, cs].reshape(1, 1, LANES) * x3
        for j in range(1, width):
            rot = pltpu.roll(x3, j, axis=1)
            pieces = []
            for s, prev8 in enumerate(prev8s):
                pieces.append(pltpu.roll(prev8[:, cs].reshape(1, SUBLANES, LANES), j, axis=1))
                if tiles_per_seg > 1:
                    pieces.append(rot[s * tiles_per_seg:(s + 1) * tiles_per_seg - 1])
            prev = pieces[0] if len(pieces) == 1 else jnp.concatenate(pieces, axis=0)
            acc = acc + w[width - 1 - j:width - j, cs].reshape(1, 1, LANES) * jnp.where(sub < j, prev, rot)
        strips.append(acc.reshape(rows, LANES))
    return strips[0] if len(strips) == 1 else jnp.concatenate(strips, axis=1)


INPROJ_SUB = 256
INPROJ_CONV_SUB = 256
INPROJ_SHORT_TILE = 512


def _inproj_kernel(*refs, tm, seg, nt, has_c0):
    if has_c0:
        (x_ref, g_ref, wh_ref, wt_ref, wdt_ref, gb_ref, cos_ref, sin_ref, cw_ref, cbias_ref, c0_ref,
         u_ref, dt_ref, co_ref, h_ref, carry_scr, acc_scr) = refs
    else:
        (x_ref, g_ref, wh_ref, wt_ref, wdt_ref, gb_ref, cos_ref, sin_ref, cw_ref, cbias_ref,
         u_ref, dt_ref, co_ref, h_ref, carry_scr, acc_scr) = refs
        c0_ref = None
    i = pl.program_id(0)
    j = pl.program_id(1)
    sub = min(INPROJ_SUB, tm)
    nseg = tm // seg

    @pl.when(j == 0)
    def _():
        for r in range(tm // sub):
            rs = slice(r * sub, (r + 1) * sub)
            hb = _rms(x_ref[rs, :], g_ref[...]).astype(BF16)
            h_ref[rs, :] = hb
            dt_ref[rs, :] = _dot(hb, wdt_ref[...])

    def tiles(epilogue, w_ref):
        for r in range(tm // sub):
            rs = slice(r * sub, (r + 1) * sub)
            u_ref[rs, :] = epilogue(_dot(h_ref[rs, :], w_ref[...]), rs).astype(BF16)

    def rotary(scale):
        def ep(acc, rs):
            cos = cos_ref[rs, :]
            sin = sin_ref[rs, :]
            half = RET_DK // 2
            out = []
            for h in range(U_TILE // RET_DK):
                x1 = acc[:, h * RET_DK:h * RET_DK + half]
                x2 = acc[:, h * RET_DK + half:(h + 1) * RET_DK]
                out += [(x1 * cos - x2 * sin) * scale, (x2 * cos + x1 * sin) * scale]
            return jnp.concatenate(out, axis=1)
        return ep

    is_silu = (j < COL_X // U_TILE) | ((j >= COL_G // U_TILE) & (j < COL_GATES // U_TILE))
    is_gate = (j >= COL_GATES // U_TILE) & (j < TILE_Q)
    is_conv = ((j >= COL_X // U_TILE) & (j < COL_V // U_TILE)) | (j == TILE_BC)
    is_plain = jnp.logical_not(is_silu | is_gate | is_conv | (j == TILE_Q) | (j == TILE_K))

    @pl.when(is_plain)
    def _():
        tiles(lambda acc, rs: acc, wt_ref)

    @pl.when(is_conv)
    def _():
        cblk = _conv_block(j)
        cw = cw_ref[...]
        cbias = cbias_ref[...]
        if nseg == 1:
            carried = carry_scr[cblk]
            init = c0_ref[0] if has_c0 else jnp.zeros_like(carried)
            prev = jnp.where(i % nt == 0, init, carried)
            csub = min(INPROJ_CONV_SUB, tm)
            for r in range(tm // csub):
                rs = slice(r * csub, (r + 1) * csub)
                slot = (j + r) % 2
                acc_scr[slot] = _dot(h_ref[rs, :], wh_ref[...])
                acc = acc_scr[slot]
                u_ref[rs, :] = _silu(_causal_conv(acc, [prev], cw, cbias)).astype(BF16)
                prev = acc[csub - SUBLANES:csub]
            carry_scr[cblk] = prev
            co_ref[0] = prev
        else:
            acc = _dot(h_ref[...], wh_ref[...])
            zero8 = jnp.zeros((SUBLANES, U_TILE), F32)
            prev8s = [c0_ref[s] if has_c0 else zero8 for s in range(nseg)]
            u_ref[...] = _silu(_causal_conv(acc, prev8s, cw, cbias)).astype(BF16)
            for s in range(nseg):
                co_ref[s] = acc[(s + 1) * seg - SUBLANES:(s + 1) * seg]

    @pl.when(is_silu & (j < TILE_V))
    def _():
        tiles(lambda acc, rs: _silu(acc), wh_ref)

    @pl.when(is_silu & (j >= TILE_V))
    def _():
        tiles(lambda acc, rs: _silu(acc), wt_ref)

    @pl.when(is_gate)
    def _():
        tiles(lambda acc, rs: _sigmoid(acc + gb_ref[...]), wt_ref)

    @pl.when(j == TILE_Q)
    def _():
        tiles(rotary(1.0), wt_ref)

    @pl.when(j == TILE_K)
    def _():
        tiles(rotary(RET_DK ** -0.5), wt_ref)


def _conv_block(j):
    return jnp.where(j >= TILE_BC, 2, jnp.where(j > COL_X // U_TILE, 1, 0))


def _in_proj(x2d, prm, cos, sin, conv0, nb, t, tm):
    n = x2d.shape[0]
    npos = cos.shape[0] // tm
    seg = min(t, tm)
    nseg = tm // seg
    nt = t // seg
    assert nseg == 1 or tm <= INPROJ_SHORT_TILE
    gate_tile0 = COL_GATES // U_TILE
    gb_map = lambda i, j: (0, jnp.clip(j - gate_tile0, 0, 2 * D_MODEL // U_TILE - 1))
    cmap = lambda i, j: (0, _conv_block(j))
    tail_spec = pl.BlockSpec((nseg, SUBLANES, U_TILE), lambda i, j: (i // nt, 0, _conv_block(j)))
    operands = [x2d, prm["norm1_g"], prm["w_head"], prm["w_tail"], prm["w_dt"], prm["gate_b"], cos, sin,
                prm["ssd_conv_w"], prm["ssd_conv_b"]]
    in_specs = [
        pl.BlockSpec((tm, D_MODEL), lambda i, j: (i, 0)),
        pl.BlockSpec((1, D_MODEL), lambda i, j: (0, 0)),
        pl.BlockSpec((D_MODEL, U_TILE), lambda i, j: (0, jnp.minimum(j, TILE_V))),
        pl.BlockSpec((D_MODEL, U_TILE), lambda i, j: (0, jnp.where(
            j < TILE_V, 2, jnp.where(j < TILE_Q, j - 2, jnp.where(j == TILE_Q, 0, 1))))),
        pl.BlockSpec((D_MODEL, DT_PAD), lambda i, j: (0, 0)),
        pl.BlockSpec((1, U_TILE), gb_map),
        pl.BlockSpec((tm, RET_DK // 2), lambda i, j: (i % npos, 0)),
        pl.BlockSpec((tm, RET_DK // 2), lambda i, j: (i % npos, 0)),
        pl.BlockSpec((SSD_CONV, U_TILE), cmap),
        pl.BlockSpec((1, U_TILE), cmap),
    ]
    if conv0 is not None:
        operands.append(conv0)
        in_specs.append(tail_spec)
    u, dt_raw, tails = pl.pallas_call(
        functools.partial(_inproj_kernel, tm=tm, seg=seg, nt=nt, has_c0=conv0 is not None),
        grid=(n // tm, U_COLS // U_TILE),
        in_specs=in_specs,
        out_specs=[
            pl.BlockSpec((tm, U_TILE), lambda i, j: (i, j)),
            pl.BlockSpec((tm, DT_PAD), lambda i, j: (i, 0)),
            pl.BlockSpec((nseg, SUBLANES, U_TILE), lambda i, j: (i, 0, _conv_block(j))),
        ],
        out_shape=[
            jax.ShapeDtypeStruct((n, U_COLS), BF16),
            jax.ShapeDtypeStruct((n, DT_PAD), F32),
            jax.ShapeDtypeStruct((nb * nt, SUBLANES, SSD_D_INNER + 2 * SSD_BC), F32),
        ],
        scratch_shapes=[pltpu.VMEM((tm, D_MODEL), BF16), pltpu.VMEM((3, SUBLANES, U_TILE), F32),
                        pltpu.VMEM((2, min(INPROJ_CONV_SUB, tm), U_TILE), F32)],
        compiler_params=_cparams(("arbitrary", "arbitrary")),
        name="in_proj",
    )(*operands)
    xbc = SSD_D_INNER + 2 * SSD_BC
    return u, dt_raw, tails.reshape(nb, nt, SUBLANES, xbc)[:, nt - 1]


def _ssd_kernel(*refs, L, Tv, has_s0):
    if has_s0:
        (x_ref, b_ref, c_ref, z_ref, dt_ref, dtb_ref, alog_ref, dsk_ref, ng_ref, e128_ref, e64_ref, s0_ref,
         y_ref, so_ref, s_scr, *pad_scr) = refs
    else:
        (x_ref, b_ref, c_ref, z_ref, dt_ref, dtb_ref, alog_ref, dsk_ref, ng_ref, e128_ref, e64_ref,
         y_ref, so_ref, s_scr, *pad_scr) = refs
    c = pl.program_id(1)
    nc = pl.num_programs(1)
    Q = LANES
    nq = L // Q

    @pl.when(c == 0)
    def _():
        if has_s0:
            s_scr[...] = s0_ref[0].reshape(SSD_D_INNER, SSD_STATE)
        else:
            s_scr[...] = jnp.zeros(s_scr.shape, F32)

    if Tv < L:
        for scr, ref in zip(pad_scr, (x_ref, b_ref, c_ref, z_ref, dt_ref)):
            scr[...] = jnp.zeros(scr.shape, scr.dtype)
            scr[0:Tv, :] = ref[...]
        x_in, b_in, c_in, z_in, dt_in = pad_scr
    else:
        x_in, b_in, c_in, z_in, dt_in = x_ref, b_ref, c_ref, z_ref, dt_ref

    rows = lax.broadcasted_iota(jnp.int32, (L, 1), 0)
    dtv = dt_in[...] + dtb_ref[...]
    dt = jnp.maximum(dtv, 0.0) + jnp.log(1.0 + jnp.exp(-jnp.abs(dtv)))
    if Tv < L:
        dt = jnp.where(rows < Tv, dt, 0.0)
    dA = dt * (-jnp.exp(alog_ref[...]))

    causal = (lax.broadcasted_iota(jnp.int32, (L, L), 0) >= lax.broadcasted_iota(jnp.int32, (L, L), 1))
    tri = jnp.where(causal, 1.0, 0.0).astype(BF16)
    hi, mid, lo = _split3(dA)
    cum = _dot(tri, hi) + _dot(tri, mid) + _dot(tri, lo)
    dec_tot = jnp.exp(cum[Tv - 1:Tv, :])
    c2 = cum * LOG2E
    r_t = (c2 - jnp.log2(dt)).T
    tri_q = (lax.broadcasted_iota(jnp.int32, (Q, Q), 0) >= lax.broadcasted_iota(jnp.int32, (Q, Q), 1))

    lane = lax.broadcasted_iota(jnp.int32, (1, LANES), 1)

    def pack3(v):
        r1 = v - v.astype(BF16).astype(F32)
        r2 = r1 - r1.astype(BF16).astype(F32)
        return jnp.where(lane < SSD_HEADS, v,
                         jnp.where(lane < 2 * SSD_HEADS, pltpu.roll(r1, SSD_HEADS, axis=1),
                                   pltpu.roll(r2, 2 * SSD_HEADS, axis=1))).astype(BF16)

    c2p = pack3(c2)
    dtp = pack3(dt)

    lo_half = lane < SSD_HEAD_DIM
    P2 = 2 * SSD_HEAD_DIM

    for g in range(SSD_GROUPS):
        gx = slice(g * SSD_GROUP_CH, (g + 1) * SSD_GROUP_CH)
        gn = slice(g * SSD_STATE, (g + 1) * SSD_STATE)
        c2_x128 = _dot(c2p, e128_ref[:, g * HEADS_PER_GROUP * LANES:(g + 1) * HEADS_PER_GROUP * LANES])
        c2_x64 = _dot(c2p, e64_ref[:, gx])
        dt_x64 = _dot(dtp, e64_ref[:, gx])
        ecum_x = jnp.exp2(c2_x64)
        todt_x = jnp.exp2(c2_x64[Tv - 1:Tv, :] - c2_x64) * dt_x64
        xcb = x_in[:, gx]
        bcb = b_in[:, gn]
        ccb = c_in[:, gn]
        xc = xcb.astype(F32)
        cb = _dot_nt(ccb, bcb)
        s_old = s_scr[gx, :]
        y_inter = _dot_nt(ccb, s_old.astype(BF16))

        y_parts = []
        xw_parts = []
        for pr in range(HEADS_PER_GROUP // 2):
            heads = [g * HEADS_PER_GROUP + 2 * pr + k for k in range(2)]
            ps = slice(pr * P2, (pr + 1) * P2)
            x_pair = xc[:, ps]
            x_pair_b = xcb[:, ps]
            zero = jnp.zeros_like(x_pair_b)
            x_a = jnp.where(lo_half, x_pair_b, zero)
            x_b = jnp.where(lo_half, zero, x_pair_b)
            y_rows = []
            for i in range(nq):
                ri = slice(i * Q, (i + 1) * Q)
                lhs, rhs = [], []
                for h, xh in zip(heads, (x_a, x_b)):
                    for jq in range(i + 1):
                        rj = slice(jq * Q, (jq + 1) * Q)
                        hg = h - g * HEADS_PER_GROUP
                        e = jnp.exp2(c2_x128[ri, hg * LANES:(hg + 1) * LANES] - r_t[h:h + 1, rj])
                        if jq == i:
                            e = jnp.where(tri_q, e, 0.0)
                        lhs.append((e * cb[ri, rj]).astype(BF16))
                    rhs.append(xh[0:(i + 1) * Q])
                y_rows.append(_dot(jnp.concatenate(lhs, axis=1), jnp.concatenate(rhs, axis=0)))
            y_pair = y_rows[0] if nq == 1 else jnp.concatenate(y_rows, axis=0)
            pg = slice(g * SSD_GROUP_CH + pr * P2, g * SSD_GROUP_CH + (pr + 1) * P2)
            y_pair = y_pair + y_inter[:, ps] * ecum_x[:, ps]
            y_pair = y_pair + x_pair * dsk_ref[:, pg]
            y_parts.append(y_pair)
            xw_parts.append((x_pair * todt_x[:, ps]).astype(BF16))

        y = jnp.concatenate(y_parts, axis=1)
        xw = jnp.concatenate(xw_parts, axis=1)

        dec_rows = [jnp.broadcast_to(dec_tot[:, g * HEADS_PER_GROUP + k:g * HEADS_PER_GROUP + k + 1],
                                     (SSD_HEAD_DIM, 1)) for k in range(HEADS_PER_GROUP)]
        s_new = s_old * jnp.concatenate(dec_rows, axis=0) + _dot_tn(xw, bcb)
        s_scr[gx, :] = s_new

        yz = y * z_in[:, gx].astype(F32)
        yn = yz * lax.rsqrt(jnp.mean(yz * yz, axis=-1, keepdims=True) + EPS) * ng_ref[:, gx]
        y_ref[:, gx] = yn[0:Tv].astype(BF16)

    @pl.when(c == nc - 1)
    def _():
        so_ref[0] = s_scr[...].reshape(SSD_HEADS, SSD_HEAD_DIM, SSD_STATE)


def _ssd(u, dt_raw, prm, s0, nb, t, L, Tv):
    nc = t // Tv
    rb = lambda b, c: b * nc + c
    full = lambda w: pl.BlockSpec(w.shape, lambda b, c: (0,) * w.ndim)
    state_spec = pl.BlockSpec((1, SSD_HEADS, SSD_HEAD_DIM, SSD_STATE), lambda b, c: (b, 0, 0, 0))
    names = ["dt_bias", "a_log", "d_skip", "ssd_norm_g", "expand128", "expand64"]
    operands = [u, u, u, u, dt_raw] + [prm[k] for k in names]
    in_specs = [
        pl.BlockSpec((Tv, SSD_D_INNER), lambda b, c: (rb(b, c), COL_X // SSD_D_INNER)),
        pl.BlockSpec((Tv, SSD_BC), lambda b, c: (rb(b, c), COL_B // SSD_BC)),
        pl.BlockSpec((Tv, SSD_BC), lambda b, c: (rb(b, c), COL_C // SSD_BC)),
        pl.BlockSpec((Tv, SSD_D_INNER), lambda b, c: (rb(b, c), COL_Z // SSD_D_INNER)),
        pl.BlockSpec((Tv, DT_PAD), lambda b, c: (rb(b, c), 0)),
    ] + [full(prm[k]) for k in names]
    if s0 is not None:
        operands.append(s0)
        in_specs.append(state_spec)
    scratch = [pltpu.VMEM((SSD_D_INNER, SSD_STATE), F32)]
    if Tv < L:
        scratch += [
            pltpu.VMEM((L, SSD_D_INNER), BF16), pltpu.VMEM((L, SSD_BC), BF16),
            pltpu.VMEM((L, SSD_BC), BF16), pltpu.VMEM((L, SSD_D_INNER), BF16),
            pltpu.VMEM((L, DT_PAD), F32),
        ]
    return pl.pallas_call(
        functools.partial(_ssd_kernel, L=L, Tv=Tv, has_s0=s0 is not None),
        grid=(nb, nc),
        in_specs=in_specs,
        out_specs=[pl.BlockSpec((Tv, SSD_D_INNER), lambda b, c: (rb(b, c), 0)), state_spec],
        out_shape=[jax.ShapeDtypeStruct((nb * t, SSD_D_INNER), BF16),
                   jax.ShapeDtypeStruct((nb, SSD_HEADS, SSD_HEAD_DIM, SSD_STATE), F32)],
        scratch_shapes=scratch,
        compiler_params=_cparams(("parallel", "arbitrary")),
        name="ssd",
    )(*operands)


def _ret_kernel(*refs, L, Tv, has_s0):
    if has_s0:
        (q_ref, k_ref, v_ref, g_ref, lg_ref, ng_ref, s0_ref,
         y_ref, so_ref, s_scr, dm_scr, cross_scr, kdec_scr, *pad_scr) = refs
    else:
        (q_ref, k_ref, v_ref, g_ref, lg_ref, ng_ref,
         y_ref, so_ref, s_scr, dm_scr, cross_scr, kdec_scr, *pad_scr) = refs
    c = pl.program_id(1)
    nc = pl.num_programs(1)

    @pl.when(c == 0)
    def _():
        if has_s0:
            s_scr[...] = s0_ref[0]
        else:
            s_scr[...] = jnp.zeros(s_scr.shape, F32)
        rowf = lax.broadcasted_iota(jnp.int32, (L, 1), 0).astype(F32)
        colf = lax.broadcasted_iota(jnp.int32, (1, L), 1).astype(F32)
        diff = rowf - colf
        for h in range(RET_HEADS):
            lg = lg_ref[h][:, 0:1]
            dm_scr[h] = jnp.where(diff >= 0.0, jnp.exp(jnp.maximum(diff, 0.0) * lg), 0.0)
            cross_scr[h] = jnp.broadcast_to(jnp.exp((rowf + 1.0) * lg), (L, LANES))
            kdec = jnp.exp((Tv - 1.0 - rowf) * lg)
            if Tv < L:
                kdec = jnp.where(rowf < Tv, kdec, 0.0)
            kdec_scr[h] = jnp.broadcast_to(kdec, (L, LANES))

    if Tv < L:
        for scr, ref in zip(pad_scr, (q_ref, k_ref, v_ref, g_ref)):
            scr[...] = jnp.zeros(scr.shape, scr.dtype)
            scr[0:Tv, :] = ref[...]
        q_in, k_in, v_in, g_in = pad_scr
    else:
        q_in, k_in, v_in, g_in = q_ref, k_ref, v_ref, g_ref

    for h in range(RET_HEADS):
        ks = slice(h * RET_DK, (h + 1) * RET_DK)
        vs = slice(h * RET_DV, (h + 1) * RET_DV)
        qh = q_in[:, ks]
        kh = k_in[:, ks]
        vh = v_in[:, vs]
        s_old = s_scr[h]
        sc = (_dot_nt(qh, kh) * dm_scr[h]).astype(BF16)
        cross = jnp.concatenate([cross_scr[h]] * (RET_DV // LANES), axis=1)
        o = _dot(sc, vh) + _dot(qh, s_old.astype(BF16)) * cross
        kdec = jnp.concatenate([kdec_scr[h]] * (RET_DK // LANES), axis=1)
        kd = (kh.astype(F32) * kdec).astype(BF16)
        s_new = s_old * jnp.exp(Tv * lg_ref[h][:, 0:1]) + _dot_tn(kd, vh)
        s_scr[h] = s_new

        mu = jnp.mean(o, axis=-1, keepdims=True)
        d = o - mu
        var = jnp.mean(d * d, axis=-1, keepdims=True)
        y = d * lax.rsqrt(var + EPS) * ng_ref[:, vs] * g_in[:, vs].astype(F32)
        y_ref[:, vs] = y[0:Tv].astype(BF16)

    @pl.when(c == nc - 1)
    def _():
        so_ref[0] = s_scr[...]


def _ret(u, lg, ng, s0, nb, t, L, Tv):
    nc = t // Tv
    rb = lambda b, c: b * nc + c
    state_spec = pl.BlockSpec((1, RET_HEADS, RET_DK, RET_DV), lambda b, c: (b, 0, 0, 0))
    in_specs = [
        pl.BlockSpec((Tv, RET_QK), lambda b, c: (rb(b, c), COL_Q // RET_QK)),
        pl.BlockSpec((Tv, RET_QK), lambda b, c: (rb(b, c), COL_K // RET_QK)),
        pl.BlockSpec((Tv, RET_V), lambda b, c: (rb(b, c), COL_V // RET_V)),
        pl.BlockSpec((Tv, RET_V), lambda b, c: (rb(b, c), COL_G // RET_V)),
        pl.BlockSpec(lg.shape, lambda b, c: (0, 0, 0)),
        pl.BlockSpec((1, RET_V), lambda b, c: (0, 0)),
    ]
    operands = [u, u, u, u, lg, ng]
    if s0 is not None:
        operands.append(s0)
        in_specs.append(state_spec)
    out_specs = [pl.BlockSpec((Tv, RET_V), lambda b, c: (rb(b, c), 0)), state_spec]
    out_shape = [
        jax.ShapeDtypeStruct((nb * t, RET_V), BF16),
        jax.ShapeDtypeStruct((nb, RET_HEADS, RET_DK, RET_DV), F32),
    ]
    scratch = [
        pltpu.VMEM((RET_HEADS, RET_DK, RET_DV), F32),
        pltpu.VMEM((RET_HEADS, L, L), F32),
        pltpu.VMEM((RET_HEADS, L, LANES), F32),
        pltpu.VMEM((RET_HEADS, L, LANES), F32),
    ]
    if Tv < L:
        scratch += [
            pltpu.VMEM((L, RET_QK), BF16), pltpu.VMEM((L, RET_QK), BF16),
            pltpu.VMEM((L, RET_V), BF16), pltpu.VMEM((L, RET_V), BF16),
        ]
    return pl.pallas_call(
        functools.partial(_ret_kernel, L=L, Tv=Tv, has_s0=s0 is not None),
        grid=(nb, nc),
        in_specs=in_specs, out_specs=out_specs, out_shape=out_shape,
        scratch_shapes=scratch,
        compiler_params=_cparams(("parallel", "arbitrary")),
        name="retention",
    )(*operands)


def _merge_kernel(x_ref, ys_ref, yr_ref, gt_ref, ws_ref, wr_ref, wo_ref, o_ref):
    a = _dot(ys_ref[...], ws_ref[...])
    b = _dot(yr_ref[...], wr_ref[...])
    mix = gt_ref[:, :D_MODEL].astype(F32) * a + gt_ref[:, D_MODEL:].astype(F32) * b
    o_ref[...] = x_ref[...] + _dot(mix.astype(BF16), wo_ref[...])


def _merge(x2d, y_ssd, y_ret, u, w_s, w_r, w_o, tm):
    n = x2d.shape[0]
    return pl.pallas_call(
        _merge_kernel,
        grid=(n // tm,),
        in_specs=[
            pl.BlockSpec((tm, D_MODEL), lambda i: (i, 0)),
            pl.BlockSpec((tm, SSD_D_INNER), lambda i: (i, 0)),
            pl.BlockSpec((tm, RET_V), lambda i: (i, 0)),
            pl.BlockSpec((tm, 2 * D_MODEL), lambda i: (i, COL_GATES // (2 * D_MODEL))),
            _resident(w_s.shape), _resident(w_r.shape), _resident(w_o.shape),
        ],
        out_specs=pl.BlockSpec((tm, D_MODEL), lambda i: (i, 0)),
        out_shape=jax.ShapeDtypeStruct((n, D_MODEL), F32),
        compiler_params=_cparams(("parallel",)),
        name="merge",
    )(x2d, y_ssd, y_ret, u, w_s, w_r, w_o)


FFN_CHUNK = 256
FFN_NCHUNK = D_FF // FFN_CHUNK


def _ffn_kernel(*refs, tm, seg, has_c0):
    if has_c0:
        (x_ref, p_ref, n2_ref, wup_ref, cw_ref, cb_ref, wdn_ref, pg_ref, wpg_ref, wpp_ref, fg_ref, c0_ref,
         y_ref, co_ref, act_scr, carry_scr) = refs
    else:
        (x_ref, p_ref, n2_ref, wup_ref, cw_ref, cb_ref, wdn_ref, pg_ref, wpg_ref, wpp_ref, fg_ref,
         y_ref, co_ref, act_scr, carry_scr) = refs
    nseg = tm // seg
    if nseg == 1:
        i = pl.program_id(1)

        @pl.when(i == 0)
        def _():
            if has_c0:
                carry_scr[...] = c0_ref[...]
            else:
                carry_scr[...] = jnp.zeros(carry_scr.shape, F32)
        prev_ref = carry_scr
    else:
        assert has_c0
        prev_ref = c0_ref

    x1 = x_ref[...]
    h2 = _rms(x1, n2_ref[...]).astype(BF16)

    for cc in range(FFN_NCHUNK):
        halves = []
        for base in (0, D_FF):
            sl = slice(base + cc * FFN_CHUNK, base + (cc + 1) * FFN_CHUNK)
            up = _dot(h2, wup_ref[:, sl])
            prev8s = [prev_ref[s, :, sl] for s in range(nseg)]
            halves.append(_causal_conv(up, prev8s, cw_ref[:, sl], cb_ref[:, sl]))
            for s in range(nseg):
                tail = up[(s + 1) * seg - SUBLANES:(s + 1) * seg]
                if nseg == 1:
                    carry_scr[s, :, sl] = tail
                else:
                    co_ref[s, :, sl] = tail
        a, b = halves
        gelu = 0.5 * a * (1.0 + lax.erf(a * (2.0 ** -0.5)))
        act_scr[:, cc * FFN_CHUNK:(cc + 1) * FFN_CHUNK] = (gelu * b).astype(BF16)

    x2 = x1 + _dot(act_scr[...], wdn_ref[...])
    hg = _rms(x2, pg_ref[...]).astype(BF16)
    gate = _sigmoid(_dot(hg, wpg_ref[...]))
    x3 = x2 + gate * _dot(p_ref[...].astype(BF16), wpp_ref[...])
    y_ref[...] = _rms(x3, fg_ref[...])

    if nseg == 1:
        @pl.when(i == pl.num_programs(1) - 1)
        def _():
            co_ref[...] = carry_scr[...]


def _ffn(x1, p2d, prm, tails, nb, t, tm):
    n = nb * t
    seg = min(t, tm)
    nseg = tm // seg
    weights = [prm[k] for k in ("norm2_g", "w_up", "ffn_conv_w", "ffn_conv_b", "w_down", "ple_norm_g",
                                "w_ple_gate", "w_ple_proj", "final_norm_g")]
    wspecs = [_resident(w.shape) for w in weights]
    nt = t // seg
    grid = (n // (tm * nt), nt)
    row = lambda b, i: (b * nt + i, 0)
    tail_spec = pl.BlockSpec((nseg, SUBLANES, 2 * D_FF), lambda b, i: (b, 0, 0))
    operands = [x1, p2d, *weights]
    in_specs = [pl.BlockSpec((tm, D_MODEL), row), pl.BlockSpec((tm, D_PLE), row)] + wspecs
    if tails is not None:
        operands.append(tails)
        in_specs.append(tail_spec)
    return pl.pallas_call(
        functools.partial(_ffn_kernel, tm=tm, seg=seg, has_c0=tails is not None),
        grid=grid,
        in_specs=in_specs,
        out_specs=[pl.BlockSpec((tm, D_MODEL), row), tail_spec],
        out_shape=[jax.ShapeDtypeStruct((n, D_MODEL), F32),
                   jax.ShapeDtypeStruct((nb, SUBLANES, 2 * D_FF), F32)],
        scratch_shapes=[pltpu.VMEM((tm, D_FF), BF16), pltpu.VMEM((1, SUBLANES, 2 * D_FF), F32)],
        compiler_params=_cparams(("parallel", "arbitrary")),
        name="ffn",
    )(*operands)


def _prep_params(norm1_g, w_in, ssd_conv_w, ssd_conv_b, dt_bias, a_log, d_skip, ssd_norm_g, w_br_ssd,
                 ret_norm_g, w_br_ret, gate_b, w_out, norm2_g, w_up, ffn_conv_w, ffn_conv_b, w_down,
                 ple_norm_g, w_ple_gate, w_ple_proj, final_norm_g):
    o = {}
    n_head = 2 * SSD_D_INNER + 2 * SSD_BC
    o["w_head"] = w_in[:, :n_head].astype(BF16)
    o["w_tail"] = w_in[:, n_head + SSD_HEADS:].astype(BF16)
    o["w_dt"] = jnp.pad(w_in[:, n_head:n_head + SSD_HEADS], ((0, 0), (0, DT_PAD - SSD_HEADS))).astype(BF16)
    o["norm1_g"] = norm1_g.reshape(1, -1)
    o["ssd_conv_w"] = ssd_conv_w
    o["ssd_conv_b"] = ssd_conv_b.reshape(1, -1)
    o["dt_bias"] = jnp.pad(dt_bias.reshape(1, -1), ((0, 0), (0, DT_PAD - SSD_HEADS)))
    o["a_log"] = jnp.pad(a_log.reshape(1, -1), ((0, 0), (0, DT_PAD - SSD_HEADS)))
    o["d_skip"] = jnp.repeat(d_skip, SSD_HEAD_DIM).reshape(1, -1)
    o["ssd_norm_g"] = ssd_norm_g.reshape(1, -1)
    o["w_br_ssd"] = w_br_ssd.astype(BF16)
    o["ret_norm_g"] = ret_norm_g.reshape(1, -1)
    o["w_br_ret"] = w_br_ret.astype(BF16)
    o["gate_b"] = gate_b.reshape(1, -1)
    o["w_out"] = w_out.astype(BF16)
    o["norm2_g"] = norm2_g.reshape(1, -1)
    o["w_up"] = w_up.astype(BF16)
    o["ffn_conv_w"] = ffn_conv_w
    o["ffn_conv_b"] = ffn_conv_b.reshape(1, -1)
    o["w_down"] = w_down.astype(BF16)
    o["ple_norm_g"] = ple_norm_g.reshape(1, -1)
    o["w_ple_gate"] = w_ple_gate.astype(BF16)
    o["w_ple_proj"] = w_ple_proj.astype(BF16)
    o["final_norm_g"] = final_norm_g.reshape(1, -1)
    piece_head = np.arange(LANES) % SSD_HEADS
    used = np.arange(LANES) < 3 * SSD_HEADS
    for width in (LANES, SSD_HEAD_DIM):
        col_head = np.arange(SSD_HEADS * width) // width
        e = (piece_head[:, None] == col_head[None, :]) & used[:, None]
        o["expand%d" % width] = jnp.asarray(e, BF16)
    lg = np.log1p(-np.power(2.0, -5.0 - np.arange(RET_HEADS)))
    o["log_gamma"] = jnp.asarray(np.broadcast_to(lg[:, None, None], (RET_HEADS, 1, LANES)), F32)
    return o


def _rope_tables(pos0, t):
    half = RET_DK // 2
    inv = np.power(ROPE_BASE, -np.arange(half) / half)
    ang = (pos0 + np.arange(t))[:, None] * inv[None, :]
    return np.cos(ang).astype(np.float32), np.sin(ang).astype(np.float32)


def _pad_tail(buf):
    return jnp.pad(buf, ((0, 0), (SUBLANES - buf.shape[1], 0), (0, 0)))


def _tiles(nb, t):
    n = nb * t
    tm_in = min(2048, t) if t >= INPROJ_SHORT_TILE else min(INPROJ_SHORT_TILE, n)
    tm_merge = min(512, n)
    tm_ffn = min(512, t) if t >= 128 else min(128, n)
    tv = min(t, MIXER_CHUNK)
    return tm_in, tm_merge, tm_ffn, tv, max(tv, MIXER_MIN_ROWS)


def _group(x, p, pos0, states, prm):
    nb, t, _ = x.shape
    n = nb * t
    x2d = x.reshape(n, D_MODEL)
    p2d = p.reshape(n, D_PLE)
    tm_in, tm_merge, tm_ffn, Tv, L = _tiles(nb, t)
    if states is None:
        conv0 = ssd_s = ret_s = ffn0 = None
    else:
        conv_buf, ssd_s, ret_s, ffn_buf = states
        conv0, ffn0 = _pad_tail(conv_buf), _pad_tail(ffn_buf)

    cos, sin = _rope_tables(pos0, t)
    if t < tm_in:
        cos, sin = np.tile(cos, (tm_in // t, 1)), np.tile(sin, (tm_in // t, 1))
    u, dt_raw, conv_tail = _in_proj(x2d, prm, jnp.asarray(cos), jnp.asarray(sin), conv0, nb, t, tm_in)
    conv_new = conv_tail[:, SUBLANES - (SSD_CONV - 1):, :]

    y_ssd, ssd_new = _ssd(u, dt_raw, prm, ssd_s, nb, t, L, Tv)
    y_ret, ret_new = _ret(u, prm["log_gamma"], prm["ret_norm_g"], ret_s, nb, t, L, Tv)
    x1 = _merge(x2d, y_ssd, y_ret, u, prm["w_br_ssd"], prm["w_br_ret"], prm["w_out"], tm_merge)
    y, ffn_tail = _ffn(x1, p2d, prm, ffn0, nb, t, tm_ffn)
    ffn_new = ffn_tail[:, SUBLANES - (FFN_CONV - 1):, :]
    return y.reshape(nb, t, D_MODEL), conv_new, ssd_new, ret_new, ffn_new


def kernel(x_prompt, x_sample, p_prompt, p_sample, state_ssd_conv, state_ssd, state_ret, state_ffn_conv,
           norm1_g, w_in, ssd_conv_w, ssd_conv_b, dt_bias, a_log, d_skip, ssd_norm_g, w_br_ssd,
           ret_norm_g, w_br_ret, gate_b, w_out, norm2_g, w_up, ffn_conv_w, ffn_conv_b, w_down,
           ple_norm_g, w_ple_gate, w_ple_proj, final_norm_g):
    assert norm1_g.shape[0] == 1, "single-layer model"
    prm = _prep_params(norm1_g[0], w_in[0], ssd_conv_w[0], ssd_conv_b[0], dt_bias[0], a_log[0], d_skip[0],
                       ssd_norm_g[0], w_br_ssd[0], ret_norm_g[0], w_br_ret[0], gate_b[0], w_out[0],
                       norm2_g[0], w_up[0], ffn_conv_w[0], ffn_conv_b[0], w_down[0], ple_norm_g[0],
                       w_ple_gate[0], w_ple_proj[0], final_norm_g)
    yp, cp, sp, rp, fp = _group(x_prompt, p_prompt[0], 0, None, prm)
    ys, cs, ss, rs, fs = _group(x_sample, p_sample[0], PAST_LEN,
                                (state_ssd_conv[0], state_ssd[0], state_ret[0], state_ffn_conv[0]), prm)
    return (yp, ys, cp[None], sp[None], rp[None], fp[None], cs[None], ss[None], rs[None], fs[None])
```

```python
import functools

import jax
import jax.numpy as jnp
import numpy as np
from jax import lax
from jax.experimental import pallas as pl
from jax.experimental.pallas import tpu as pltpu

F32 = jnp.float32
BF16 = jnp.bfloat16

EPS = 1e-6
D_MODEL = 1024
D_PLE = 256
SSD_D_INNER = 2048
SSD_HEAD_DIM = 64
SSD_HEADS = 32
SSD_GROUPS = 4
SSD_STATE = 128
SSD_CONV = 4
SSD_BC = SSD_GROUPS * SSD_STATE
SSD_GROUP_CH = SSD_D_INNER // SSD_GROUPS
HEADS_PER_GROUP = SSD_HEADS // SSD_GROUPS
RET_HEADS = 4
RET_DK = 256
RET_DV = 512
RET_QK = RET_HEADS * RET_DK
RET_V = RET_HEADS * RET_DV
ROPE_BASE = 10000.0
D_FF = 2816
FFN_CONV = 3
PAST_LEN = 1024

U_TILE = 1024
U_COLS = 13 * U_TILE
COL_Z, COL_X, COL_V, COL_G, COL_GATES = 0, 2048, 4096, 6144, 8192
COL_Q, COL_K, COL_B, COL_C = 10240, 11264, 12288, 12800
TILE_V, TILE_Q, TILE_K, TILE_BC = COL_V // U_TILE, COL_Q // U_TILE, COL_K // U_TILE, COL_B // U_TILE
DT_PAD = 128

LANES = 128
SUBLANES = 8
MIXER_CHUNK = 256
MIXER_MIN_ROWS = 128
VMEM_LIMIT = 56 * 1024 * 1024
LOG2E = 1.4426950408889634


def _cparams(sem):
    return pltpu.CompilerParams(dimension_semantics=sem, vmem_limit_bytes=VMEM_LIMIT)


def _resident(shape):
    nd = len(shape)
    return pl.BlockSpec(shape, lambda *_: (0,) * nd, pipeline_mode=pl.Buffered(1))


def _rms(x, g):
    return x * lax.rsqrt(jnp.mean(x * x, axis=-1, keepdims=True) + EPS) * g


def _sigmoid(x):
    return 0.5 + 0.5 * jnp.tanh(0.5 * x)


def _silu(x):
    h = 0.5 * x
    return h + h * jnp.tanh(h)


def _split3(x):
    hi = x.astype(BF16)
    r1 = x - hi.astype(F32)
    mid = r1.astype(BF16)
    lo = (r1 - mid.astype(F32)).astype(BF16)
    return hi, mid, lo


def _dot(a, b):
    return jnp.dot(a, b, preferred_element_type=F32)


def _dot_nt(a, b):
    return lax.dot_general(a, b, (((1,), (1,)), ((), ())), preferred_element_type=F32)


def _dot_tn(a, b):
    return lax.dot_general(a, b, (((0,), (0,)), ((), ())), preferred_element_type=F32)


def _causal_conv(pre, prev8s, w, b):
    width = w.shape[0]
    rows, ch = pre.shape
    tiles_per_seg = rows // len(prev8s) // SUBLANES
    sub = lax.broadcasted_iota(jnp.int32, (1, SUBLANES, 1), 1)
    strips = []
    for c0 in range(0, ch, LANES):
        cs = slice(c0, c0 + LANES)
        x3 = pre[:, cs].reshape(rows // SUBLANES, SUBLANES, LANES)
        acc = b[:, cs].reshape(1, 1, LANES) + w[width - 1:width, cs].reshape(1, 1, LANES) * x3
        for j in range(1, width):
            rot = pltpu.roll(x3, j, axis=1)
            pieces = []
            for s, prev8 in enumerate(prev8s):
                pieces.append(pltpu.roll(prev8[:, cs].reshape(1, SUBLANES, LANES), j, axis=1))
                if tiles_per_seg > 1:
                    pieces.append(rot[s * tiles_per_seg:(s + 1) * tiles_per_seg - 1])
            prev = pieces[0] if len(pieces) == 1 else jnp.concatenate(pieces, axis=0)
            acc = acc + w[width - 1 - j:width - j, cs].reshape(1, 1, LANES) * jnp.where(sub < j, prev, rot)
        strips.append(acc.reshape(rows, LANES))
    return strips[0] if len(strips) == 1 else jnp.concatenate(strips, axis=1)


INPROJ_SUB = 256
INPROJ_CONV_SUB = 256
INPROJ_SHORT_TILE = 512


def _inproj_kernel(*refs, tm, seg, nt, has_c0):
    if has_c0:
        (x_ref, g_ref, wh_ref, wt_ref, wdt_ref, gb_ref, cos_ref, sin_ref, cw_ref, cbias_ref, c0_ref,
         u_ref, dt_ref, co_ref, h_ref, carry_scr, acc_scr) = refs
    else:
        (x_ref, g_ref, wh_ref, wt_ref, wdt_ref, gb_ref, cos_ref, sin_ref, cw_ref, cbias_ref,
         u_ref, dt_ref, co_ref, h_ref, carry_scr, acc_scr) = refs
        c0_ref = None
    i = pl.program_id(0)
    j = pl.program_id(1)
    sub = min(INPROJ_SUB, tm)
    nseg = tm // seg

    @pl.when(j == 0)
    def _():
        for r in range(tm // sub):
            rs = slice(r * sub, (r + 1) * sub)
            hb = _rms(x_ref[rs, :], g_ref[...]).astype(BF16)
            h_ref[rs, :] = hb
            dt_ref[rs, :] = _dot(hb, wdt_ref[...])

    def tiles(epilogue, w_ref):
        w = w_ref[...].astype(BF16)
        for r in range(tm // sub):
            rs = slice(r * sub, (r + 1) * sub)
            u_ref[rs, :] = epilogue(_dot(h_ref[rs, :], w), rs).astype(BF16)

    def rotary(scale):
        def ep(acc, rs):
            cos = cos_ref[rs, :]
            sin = sin_ref[rs, :]
            half = RET_DK // 2
            out = []
            for h in range(U_TILE // RET_DK):
                x1 = acc[:, h * RET_DK:h * RET_DK + half]
                x2 = acc[:, h * RET_DK + half:(h + 1) * RET_DK]
                out += [(x1 * cos - x2 * sin) * scale, (x2 * cos + x1 * sin) * scale]
            return jnp.concatenate(out, axis=1)
        return ep

    is_silu = (j < COL_X // U_TILE) | ((j >= COL_G // U_TILE) & (j < COL_GATES // U_TILE))
    is_gate = (j >= COL_GATES // U_TILE) & (j < TILE_Q)
    is_conv = ((j >= COL_X // U_TILE) & (j < COL_V // U_TILE)) | (j == TILE_BC)
    is_plain = jnp.logical_not(is_silu | is_gate | is_conv | (j == TILE_Q) | (j == TILE_K))

    @pl.when(is_plain)
    def _():
        tiles(lambda acc, rs: acc, wt_ref)

    @pl.when(is_conv)
    def _():
        cblk = _conv_block(j)
        cw = cw_ref[...]
        cbias = cbias_ref[...]
        wh = wh_ref[...].astype(BF16)
        if nseg == 1:
            carried = carry_scr[cblk]
            init = c0_ref[0] if has_c0 else jnp.zeros_like(carried)
            prev = jnp.where(i % nt == 0, init, carried)
            csub = min(INPROJ_CONV_SUB, tm)
            for r in range(tm // csub):
                rs = slice(r * csub, (r + 1) * csub)
                slot = (j + r) % 2
                acc_scr[slot] = _dot(h_ref[rs, :], wh)
                acc = acc_scr[slot]
                u_ref[rs, :] = _silu(_causal_conv(acc, [prev], cw, cbias)).astype(BF16)
                prev = acc[csub - SUBLANES:csub]
            carry_scr[cblk] = prev
            co_ref[0] = prev
        else:
            acc = _dot(h_ref[...], wh)
            zero8 = jnp.zeros((SUBLANES, U_TILE), F32)
            prev8s = [c0_ref[s] if has_c0 else zero8 for s in range(nseg)]
            u_ref[...] = _silu(_causal_conv(acc, prev8s, cw, cbias)).astype(BF16)
            for s in range(nseg):
                co_ref[s] = acc[(s + 1) * seg - SUBLANES:(s + 1) * seg]

    @pl.when(is_silu & (j < TILE_V))
    def _():
        tiles(lambda acc, rs: _silu(acc), wh_ref)

    @pl.when(is_silu & (j >= TILE_V))
    def _():
        tiles(lambda acc, rs: _silu(acc), wt_ref)

    @pl.when(is_gate)
    def _():
        tiles(lambda acc, rs: _sigmoid(acc + gb_ref[...]), wt_ref)

    @pl.when(j == TILE_Q)
    def _():
        tiles(rotary(1.0), wt_ref)

    @pl.when(j == TILE_K)
    def _():
        tiles(rotary(RET_DK ** -0.5), wt_ref)


def _conv_block(j):
    return jnp.where(j >= TILE_BC, 2, jnp.where(j > COL_X // U_TILE, 1, 0))


def _in_proj(x2d, prm, cos, sin, conv0, nb, t, tm):
    n = x2d.shape[0]
    npos = cos.shape[0] // tm
    seg = min(t, tm)
    nseg = tm // seg
    nt = t // seg
    assert nseg == 1 or tm <= INPROJ_SHORT_TILE
    gate_tile0 = COL_GATES // U_TILE
    gb_map = lambda i, j: (0, jnp.clip(j - gate_tile0, 0, 2 * D_MODEL // U_TILE - 1))
    cmap = lambda i, j: (0, _conv_block(j))
    tail_spec = pl.BlockSpec((nseg, SUBLANES, U_TILE), lambda i, j: (i // nt, 0, _conv_block(j)))
    operands = [x2d, prm["norm1_g"], prm["w_head"], prm["w_tail"], prm["w_dt"], prm["gate_b"], cos, sin,
                prm["ssd_conv_w"], prm["ssd_conv_b"]]
    in_specs = [
        pl.BlockSpec((tm, D_MODEL), lambda i, j: (i, 0)),
        pl.BlockSpec((1, D_MODEL), lambda i, j: (0, 0)),
        pl.BlockSpec((D_MODEL, U_TILE), lambda i, j: (0, jnp.minimum(j, TILE_V))),
        pl.BlockSpec((D_MODEL, U_TILE), lambda i, j: (0, jnp.where(
            j < TILE_V, 2, jnp.where(j < TILE_Q, j - 2, jnp.where(j == TILE_Q, 0, 1))))),
        pl.BlockSpec((D_MODEL, DT_PAD), lambda i, j: (0, 0)),
        pl.BlockSpec((1, U_TILE), gb_map),
        pl.BlockSpec((tm, RET_DK // 2), lambda i, j: (i % npos, 0)),
        pl.BlockSpec((tm, RET_DK // 2), lambda i, j: (i % npos, 0)),
        pl.BlockSpec((SSD_CONV, U_TILE), cmap),
        pl.BlockSpec((1, U_TILE), cmap),
    ]
    if conv0 is not None:
        operands.append(conv0)
        in_specs.append(tail_spec)
    u, dt_raw, tails = pl.pallas_call(
        functools.partial(_inproj_kernel, tm=tm, seg=seg, nt=nt, has_c0=conv0 is not None),
        grid=(n // tm, U_COLS // U_TILE),
        in_specs=in_specs,
        out_specs=[
            pl.BlockSpec((tm, U_TILE), lambda i, j: (i, j)),
            pl.BlockSpec((tm, DT_PAD), lambda i, j: (i, 0)),
            pl.BlockSpec((nseg, SUBLANES, U_TILE), lambda i, j: (i, 0, _conv_block(j))),
        ],
        out_shape=[
            jax.ShapeDtypeStruct((n, U_COLS), BF16),
            jax.ShapeDtypeStruct((n, DT_PAD), F32),
            jax.ShapeDtypeStruct((nb * nt, SUBLANES, SSD_D_INNER + 2 * SSD_BC), F32),
        ],
        scratch_shapes=[pltpu.VMEM((tm, D_MODEL), BF16), pltpu.VMEM((3, SUBLANES, U_TILE), F32),
                        pltpu.VMEM((2, min(INPROJ_CONV_SUB, tm), U_TILE), F32)],
        compiler_params=_cparams(("arbitrary", "arbitrary")),
        name="in_proj",
    )(*operands)
    xbc = SSD_D_INNER + 2 * SSD_BC
    return u, dt_raw, tails.reshape(nb, nt, SUBLANES, xbc)[:, nt - 1]


def _ssd_kernel(*refs, L, Tv, has_s0):
    if has_s0:
        (x_ref, b_ref, c_ref, z_ref, dt_ref, dtb_ref, alog_ref, dsk_ref, ng_ref, e64_ref, s0_ref,
         y_ref, so_ref, s_scr, *pad_scr) = refs
    else:
        (x_ref, b_ref, c_ref, z_ref, dt_ref, dtb_ref, alog_ref, dsk_ref, ng_ref, e64_ref,
         y_ref, so_ref, s_scr, *pad_scr) = refs
    c = pl.program_id(1)
    nc = pl.num_programs(1)
    Q = LANES
    nq = L // Q

    @pl.when(c == 0)
    def _():
        if has_s0:
            s_scr[...] = s0_ref[0].reshape(SSD_D_INNER, SSD_STATE)
        else:
            s_scr[...] = jnp.zeros(s_scr.shape, F32)

    if Tv < L:
        for scr, ref in zip(pad_scr, (x_ref, b_ref, c_ref, z_ref, dt_ref)):
            scr[...] = jnp.zeros(scr.shape, scr.dtype)
            scr[0:Tv, :] = ref[...]
        x_in, b_in, c_in, z_in, dt_in = pad_scr
    else:
        x_in, b_in, c_in, z_in, dt_in = x_ref, b_ref, c_ref, z_ref, dt_ref

    rows = lax.broadcasted_iota(jnp.int32, (L, 1), 0)
    dtv = dt_in[...] + dtb_ref[...]
    dt = jnp.maximum(dtv, 0.0) + jnp.log(1.0 + jnp.exp(-jnp.abs(dtv)))
    if Tv < L:
        dt = jnp.where(rows < Tv, dt, 0.0)
    dA = dt * (-jnp.exp(alog_ref[...]))

    causal = (lax.broadcasted_iota(jnp.int32, (L, L), 0) >= lax.broadcasted_iota(jnp.int32, (L, L), 1))
    tri = jnp.where(causal, 1.0, 0.0).astype(BF16)
    hi, mid, lo = _split3(dA)
    cum = _dot(tri, hi) + _dot(tri, mid) + _dot(tri, lo)
    dec_tot = jnp.exp(cum[Tv - 1:Tv, :])
    c2 = cum * LOG2E
    r_t = (c2 - jnp.log2(dt)).T
    tri_q = (lax.broadcasted_iota(jnp.int32, (Q, Q), 0) >= lax.broadcasted_iota(jnp.int32, (Q, Q), 1))

    lane = lax.broadcasted_iota(jnp.int32, (1, LANES), 1)

    def pack3(v):
        r1 = v - v.astype(BF16).astype(F32)
        r2 = r1 - r1.astype(BF16).astype(F32)
        return jnp.where(lane < SSD_HEADS, v,
                         jnp.where(lane < 2 * SSD_HEADS, pltpu.roll(r1, SSD_HEADS, axis=1),
                                   pltpu.roll(r2, 2 * SSD_HEADS, axis=1))).astype(BF16)

    c2p = pack3(c2)
    dtp = pack3(dt)

    lo_half = lane < SSD_HEAD_DIM
    P2 = 2 * SSD_HEAD_DIM

    for g in range(SSD_GROUPS):
        gx = slice(g * SSD_GROUP_CH, (g + 1) * SSD_GROUP_CH)
        gn = slice(g * SSD_STATE, (g + 1) * SSD_STATE)
        c2_x64 = _dot(c2p, e64_ref[:, gx])
        dt_x64 = _dot(dtp, e64_ref[:, gx])
        ecum_x = jnp.exp2(c2_x64)
        todt_x = jnp.exp2(c2_x64[Tv - 1:Tv, :] - c2_x64) * dt_x64
        xcb = x_in[:, gx]
        bcb = b_in[:, gn]
        ccb = c_in[:, gn]
        xc = xcb.astype(F32)
        cb = _dot_nt(ccb, bcb)
        s_old = s_scr[gx, :]
        y_inter = _dot_nt(ccb, s_old.astype(BF16))

        y_parts = []
        xw_parts = []
        for pr in range(HEADS_PER_GROUP // 2):
            heads = [g * HEADS_PER_GROUP + 2 * pr + k for k in range(2)]
            ps = slice(pr * P2, (pr + 1) * P2)
            x_pair = xc[:, ps]
            x_pair_b = xcb[:, ps]
            zero = jnp.zeros_like(x_pair_b)
            x_a = jnp.where(lo_half, x_pair_b, zero)
            x_b = jnp.where(lo_half, zero, x_pair_b)
            y_rows = []
            for i in range(nq):
                ri = slice(i * Q, (i + 1) * Q)
                lhs, rhs = [], []
                for h, xh in zip(heads, (x_a, x_b)):
                    for jq in range(i + 1):
                        rj = slice(jq * Q, (jq + 1) * Q)
                        e = jnp.exp2(c2[ri, h:h + 1] - r_t[h:h + 1, rj])
                        if jq == i:
                            e = jnp.where(tri_q, e, 0.0)
                        lhs.append((e * cb[ri, rj]).astype(BF16))
                    rhs.append(xh[0:(i + 1) * Q])
                y_rows.append(_dot(jnp.concatenate(lhs, axis=1), jnp.concatenate(rhs, axis=0)))
            y_pair = y_rows[0] if nq == 1 else jnp.concatenate(y_rows, axis=0)
            pg = slice(g * SSD_GROUP_CH + pr * P2, g * SSD_GROUP_CH + (pr + 1) * P2)
            y_pair = y_pair + y_inter[:, ps] * ecum_x[:, ps]
            y_pair = y_pair + x_pair * dsk_ref[:, pg]
            y_parts.append(y_pair)
            xw_parts.append((x_pair * todt_x[:, ps]).astype(BF16))

        y = jnp.concatenate(y_parts, axis=1)
        xw = jnp.concatenate(xw_parts, axis=1)

        dec_rows = [jnp.broadcast_to(dec_tot[:, g * HEADS_PER_GROUP + k:g * HEADS_PER_GROUP + k + 1],
                                     (SSD_HEAD_DIM, 1)) for k in range(HEADS_PER_GROUP)]
        s_new = s_old * jnp.concatenate(dec_rows, axis=0) + _dot_tn(xw, bcb)
        s_scr[gx, :] = s_new

        yz = y * z_in[:, gx].astype(F32)
        yn = yz * lax.rsqrt(jnp.mean(yz * yz, axis=-1, keepdims=True) + EPS) * ng_ref[:, gx]
        y_ref[:, gx] = yn[0:Tv].astype(BF16)

    @pl.when(c == nc - 1)
    def _():
        so_ref[0] = s_scr[...].reshape(SSD_HEADS, SSD_HEAD_DIM, SSD_STATE)


def _ssd(u, dt_raw, prm, s0, nb, t, L, Tv):
    nc = t // Tv
    rb = lambda b, c: b * nc + c
    full = lambda w: pl.BlockSpec(w.shape, lambda b, c: (0,) * w.ndim)
    state_spec = pl.BlockSpec((1, SSD_HEADS, SSD_HEAD_DIM, SSD_STATE), lambda b, c: (b, 0, 0, 0))
    names = ["dt_bias", "a_log", "d_skip", "ssd_norm_g", "expand64"]
    operands = [u, u, u, u, dt_raw] + [prm[k] for k in names]
    in_specs = [
        pl.BlockSpec((Tv, SSD_D_INNER), lambda b, c: (rb(b, c), COL_X // SSD_D_INNER)),
        pl.BlockSpec((Tv, SSD_BC), lambda b, c: (rb(b, c), COL_B // SSD_BC)),
        pl.BlockSpec((Tv, SSD_BC), lambda b, c: (rb(b, c), COL_C // SSD_BC)),
        pl.BlockSpec((Tv, SSD_D_INNER), lambda b, c: (rb(b, c), COL_Z // SSD_D_INNER)),
        pl.BlockSpec((Tv, DT_PAD), lambda b, c: (rb(b, c), 0)),
    ] + [full(prm[k]) for k in names]
    if s0 is not None:
        operands.append(s0)
        in_specs.append(state_spec)
    scratch = [pltpu.VMEM((SSD_D_INNER, SSD_STATE), F32)]
    if Tv < L:
        scratch += [
            pltpu.VMEM((L, SSD_D_INNER), BF16), pltpu.VMEM((L, SSD_BC), BF16),
            pltpu.VMEM((L, SSD_BC), BF16), pltpu.VMEM((L, SSD_D_INNER), BF16),
            pltpu.VMEM((L, DT_PAD), F32),
        ]
    return pl.pallas_call(
        functools.partial(_ssd_kernel, L=L, Tv=Tv, has_s0=s0 is not None),
        grid=(nb, nc),
        in_specs=in_specs,
        out_specs=[pl.BlockSpec((Tv, SSD_D_INNER), lambda b, c: (rb(b, c), 0)), state_spec],
        out_shape=[jax.ShapeDtypeStruct((nb * t, SSD_D_INNER), BF16),
                   jax.ShapeDtypeStruct((nb, SSD_HEADS, SSD_HEAD_DIM, SSD_STATE), F32)],
        scratch_shapes=scratch,
        compiler_params=_cparams(("parallel", "arbitrary")),
        name="ssd",
    )(*operands)


def _ret_kernel(*refs, L, Tv, has_s0):
    if has_s0:
        (q_ref, k_ref, v_ref, g_ref, lg_ref, ng_ref, s0_ref,
         y_ref, so_ref, s_scr, dm_scr, cross_scr, kdec_scr, *pad_scr) = refs
    else:
        (q_ref, k_ref, v_ref, g_ref, lg_ref, ng_ref,
         y_ref, so_ref, s_scr, dm_scr, cross_scr, kdec_scr, *pad_scr) = refs
    c = pl.program_id(1)
    nc = pl.num_programs(1)

    @pl.when(c == 0)
    def _():
        if has_s0:
            s_scr[...] = s0_ref[0]
        else:
            s_scr[...] = jnp.zeros(s_scr.shape, F32)
        rowf = lax.broadcasted_iota(jnp.int32, (L, 1), 0).astype(F32)
        colf = lax.broadcasted_iota(jnp.int32, (1, L), 1).astype(F32)
        diff = rowf - colf
        for h in range(RET_HEADS):
            lg = lg_ref[h][:, 0:1]
            dm_scr[h] = jnp.where(diff >= 0.0, jnp.exp(jnp.maximum(diff, 0.0) * lg), 0.0)
            cross_scr[h] = jnp.broadcast_to(jnp.exp((rowf + 1.0) * lg), (L, LANES))
            kdec = jnp.exp((Tv - 1.0 - rowf) * lg)
            if Tv < L:
                kdec = jnp.where(rowf < Tv, kdec, 0.0)
            kdec_scr[h] = jnp.broadcast_to(kdec, (L, LANES))

    if Tv < L:
        for scr, ref in zip(pad_scr, (q_ref, k_ref, v_ref, g_ref)):
            scr[...] = jnp.zeros(scr.shape, scr.dtype)
            scr[0:Tv, :] = ref[...]
        q_in, k_in, v_in, g_in = pad_scr
    else:
        q_in, k_in, v_in, g_in = q_ref, k_ref, v_ref, g_ref

    for h in range(RET_HEADS):
        ks = slice(h * RET_DK, (h + 1) * RET_DK)
        vs = slice(h * RET_DV, (h + 1) * RET_DV)
        qh = q_in[:, ks]
        kh = k_in[:, ks]
        vh = v_in[:, vs]
        s_old = s_scr[h]
        sc = (_dot_nt(qh, kh) * dm_scr[h]).astype(BF16)
        cross = jnp.concatenate([cross_scr[h]] * (RET_DV // LANES), axis=1)
        o = _dot(sc, vh) + _dot(qh, s_old.astype(BF16)) * cross
        kdec = jnp.concatenate([kdec_scr[h]] * (RET_DK // LANES), axis=1)
        kd = (kh.astype(F32) * kdec).astype(BF16)
        s_new = s_old * jnp.exp(Tv * lg_ref[h][:, 0:1]) + _dot_tn(kd, vh)
        s_scr[h] = s_new

        mu = jnp.mean(o, axis=-1, keepdims=True)
        d = o - mu
        var = jnp.mean(d * d, axis=-1, keepdims=True)
        y = d * lax.rsqrt(var + EPS) * ng_ref[:, vs] * g_in[:, vs].astype(F32)
        y_ref[:, vs] = y[0:Tv].astype(BF16)

    @pl.when(c == nc - 1)
    def _():
        so_ref[0] = s_scr[...]


def _ret(u, lg, ng, s0, nb, t, L, Tv):
    nc = t // Tv
    rb = lambda b, c: b * nc + c
    state_spec = pl.BlockSpec((1, RET_HEADS, RET_DK, RET_DV), lambda b, c: (b, 0, 0, 0))
    in_specs = [
        pl.BlockSpec((Tv, RET_QK), lambda b, c: (rb(b, c), COL_Q // RET_QK)),
        pl.BlockSpec((Tv, RET_QK), lambda b, c: (rb(b, c), COL_K // RET_QK)),
        pl.BlockSpec((Tv, RET_V), lambda b, c: (rb(b, c), COL_V // RET_V)),
        pl.BlockSpec((Tv, RET_V), lambda b, c: (rb(b, c), COL_G // RET_V)),
        pl.BlockSpec(lg.shape, lambda b, c: (0, 0, 0)),
        pl.BlockSpec((1, RET_V), lambda b, c: (0, 0)),
    ]
    operands = [u, u, u, u, lg, ng]
    if s0 is not None:
        operands.append(s0)
        in_specs.append(state_spec)
    out_specs = [pl.BlockSpec((Tv, RET_V), lambda b, c: (rb(b, c), 0)), state_spec]
    out_shape = [
        jax.ShapeDtypeStruct((nb * t, RET_V), BF16),
        jax.ShapeDtypeStruct((nb, RET_HEADS, RET_DK, RET_DV), F32),
    ]
    scratch = [
        pltpu.VMEM((RET_HEADS, RET_DK, RET_DV), F32),
        pltpu.VMEM((RET_HEADS, L, L), F32),
        pltpu.VMEM((RET_HEADS, L, LANES), F32),
        pltpu.VMEM((RET_HEADS, L, LANES), F32),
    ]
    if Tv < L:
        scratch += [
            pltpu.VMEM((L, RET_QK), BF16), pltpu.VMEM((L, RET_QK), BF16),
            pltpu.VMEM((L, RET_V), BF16), pltpu.VMEM((L, RET_V), BF16),
        ]
    return pl.pallas_call(
        functools.partial(_ret_kernel, L=L, Tv=Tv, has_s0=s0 is not None),
        grid=(nb, nc),
        in_specs=in_specs, out_specs=out_specs, out_shape=out_shape,
        scratch_shapes=scratch,
        compiler_params=_cparams(("parallel", "arbitrary")),
        name="retention",
    )(*operands)


def _merge_kernel(x_ref, ys_ref, yr_ref, gt_ref, ws_ref, wr_ref, wo_ref, o_ref):
    a = _dot(ys_ref[...], ws_ref[...])
    b = _dot(yr_ref[...], wr_ref[...])
    mix = gt_ref[:, :D_MODEL].astype(F32) * a + gt_ref[:, D_MODEL:].astype(F32) * b
    o_ref[...] = x_ref[...] + _dot(mix.astype(BF16), wo_ref[...])


def _merge(x2d, y_ssd, y_ret, u, w_s, w_r, w_o, tm):
    n = x2d.shape[0]
    return pl.pallas_call(
        _merge_kernel,
        grid=(n // tm,),
        in_specs=[
            pl.BlockSpec((tm, D_MODEL), lambda i: (i, 0)),
            pl.BlockSpec((tm, SSD_D_INNER), lambda i: (i, 0)),
            pl.BlockSpec((tm, RET_V), lambda i: (i, 0)),
            pl.BlockSpec((tm, 2 * D_MODEL), lambda i: (i, COL_GATES // (2 * D_MODEL))),
            _resident(w_s.shape), _resident(w_r.shape), _resident(w_o.shape),
        ],
        out_specs=pl.BlockSpec((tm, D_MODEL), lambda i: (i, 0)),
        out_shape=jax.ShapeDtypeStruct((n, D_MODEL), F32),
        compiler_params=_cparams(("parallel",)),
        name="merge",
    )(x2d, y_ssd, y_ret, u, w_s, w_r, w_o)


FFN_CHUNK = 256
FFN_NCHUNK = D_FF // FFN_CHUNK


def _ffn_kernel(*refs, tm, seg, has_c0):
    if has_c0:
        (x_ref, p_ref, n2_ref, wup_ref, cw_ref, cb_ref, wdn_ref, pg_ref, wpg_ref, wpp_ref, fg_ref, c0_ref,
         y_ref, co_ref, act_scr, carry_scr) = refs
    else:
        (x_ref, p_ref, n2_ref, wup_ref, cw_ref, cb_ref, wdn_ref, pg_ref, wpg_ref, wpp_ref, fg_ref,
         y_ref, co_ref, act_scr, carry_scr) = refs
    nseg = tm // seg
    if nseg == 1:
        i = pl.program_id(1)

        @pl.when(i == 0)
        def _():
            if has_c0:
                carry_scr[...] = c0_ref[...]
            else:
                carry_scr[...] = jnp.zeros(carry_scr.shape, F32)
        prev_ref = carry_scr
    else:
        assert has_c0
        prev_ref = c0_ref

    x1 = x_ref[...]
    h2 = _rms(x1, n2_ref[...]).astype(BF16)

    for cc in range(FFN_NCHUNK):
        halves = []
        for base in (0, D_FF):
            sl = slice(base + cc * FFN_CHUNK, base + (cc + 1) * FFN_CHUNK)
            up = _dot(h2, wup_ref[:, sl])
            prev8s = [prev_ref[s, :, sl] for s in range(nseg)]
            halves.append(_causal_conv(up, prev8s, cw_ref[:, sl], cb_ref[:, sl]))
            for s in range(nseg):
                tail = up[(s + 1) * seg - SUBLANES:(s + 1) * seg]
                if nseg == 1:
                    carry_scr[s, :, sl] = tail
                else:
                    co_ref[s, :, sl] = tail
        a, b = halves
        gelu = 0.5 * a * (1.0 + lax.erf(a * (2.0 ** -0.5)))
        act_scr[:, cc * FFN_CHUNK:(cc + 1) * FFN_CHUNK] = (gelu * b).astype(BF16)

    x2 = x1 + _dot(act_scr[...], wdn_ref[...])
    hg = _rms(x2, pg_ref[...]).astype(BF16)
    gate = _sigmoid(_dot(hg, wpg_ref[...]))
    x3 = x2 + gate * _dot(p_ref[...].astype(BF16), wpp_ref[...])
    y_ref[...] = _rms(x3, fg_ref[...])

    if nseg == 1:
        @pl.when(i == pl.num_programs(1) - 1)
        def _():
            co_ref[...] = carry_scr[...]


def _ffn(x1, p2d, prm, tails, nb, t, tm):
    n = nb * t
    seg = min(t, tm)
    nseg = tm // seg
    weights = [prm[k] for k in ("norm2_g", "w_up", "ffn_conv_w", "ffn_conv_b", "w_down", "ple_norm_g",
                                "w_ple_gate", "w_ple_proj", "final_norm_g")]
    wspecs = [_resident(w.shape) for w in weights]
    nt = t // seg
    grid = (n // (tm * nt), nt)
    row = lambda b, i: (b * nt + i, 0)
    tail_spec = pl.BlockSpec((nseg, SUBLANES, 2 * D_FF), lambda b, i: (b, 0, 0))
    operands = [x1, p2d, *weights]
    in_specs = [pl.BlockSpec((tm, D_MODEL), row), pl.BlockSpec((tm, D_PLE), row)] + wspecs
    if tails is not None:
        operands.append(tails)
        in_specs.append(tail_spec)
    return pl.pallas_call(
        functools.partial(_ffn_kernel, tm=tm, seg=seg, has_c0=tails is not None),
        grid=grid,
        in_specs=in_specs,
        out_specs=[pl.BlockSpec((tm, D_MODEL), row), tail_spec],
        out_shape=[jax.ShapeDtypeStruct((n, D_MODEL), F32),
                   jax.ShapeDtypeStruct((nb, SUBLANES, 2 * D_FF), F32)],
        scratch_shapes=[pltpu.VMEM((tm, D_FF), BF16), pltpu.VMEM((1, SUBLANES, 2 * D_FF), F32)],
        compiler_params=_cparams(("parallel", "arbitrary")),
        name="ffn",
    )(*operands)


def _prep_params(norm1_g, w_in, ssd_conv_w, ssd_conv_b, dt_bias, a_log, d_skip, ssd_norm_g, w_br_ssd,
                 ret_norm_g, w_br_ret, gate_b, w_out, norm2_g, w_up, ffn_conv_w, ffn_conv_b, w_down,
                 ple_norm_g, w_ple_gate, w_ple_proj, final_norm_g):
    o = {}
    n_head = 2 * SSD_D_INNER + 2 * SSD_BC
    o["w_head"] = w_in
    o["w_tail"] = w_in[:, n_head + SSD_HEADS:].astype(BF16)
    o["w_dt"] = jnp.pad(w_in[:, n_head:n_head + SSD_HEADS], ((0, 0), (0, DT_PAD - SSD_HEADS))).astype(BF16)
    o["norm1_g"] = norm1_g.reshape(1, -1)
    o["ssd_conv_w"] = ssd_conv_w
    o["ssd_conv_b"] = ssd_conv_b.reshape(1, -1)
    o["dt_bias"] = jnp.pad(dt_bias.reshape(1, -1), ((0, 0), (0, DT_PAD - SSD_HEADS)))
    o["a_log"] = jnp.pad(a_log.reshape(1, -1), ((0, 0), (0, DT_PAD - SSD_HEADS)))
    o["d_skip"] = jnp.repeat(d_skip, SSD_HEAD_DIM).reshape(1, -1)
    o["ssd_norm_g"] = ssd_norm_g.reshape(1, -1)
    o["w_br_ssd"] = w_br_ssd.astype(BF16)
    o["ret_norm_g"] = ret_norm_g.reshape(1, -1)
    o["w_br_ret"] = w_br_ret.astype(BF16)
    o["gate_b"] = gate_b.reshape(1, -1)
    o["w_out"] = w_out.astype(BF16)
    o["norm2_g"] = norm2_g.reshape(1, -1)
    o["w_up"] = w_up.astype(BF16)
    o["ffn_conv_w"] = ffn_conv_w
    o["ffn_conv_b"] = ffn_conv_b.reshape(1, -1)
    o["w_down"] = w_down.astype(BF16)
    o["ple_norm_g"] = ple_norm_g.reshape(1, -1)
    o["w_ple_gate"] = w_ple_gate.astype(BF16)
    o["w_ple_proj"] = w_ple_proj.astype(BF16)
    o["final_norm_g"] = final_norm_g.reshape(1, -1)
    piece_head = np.arange(LANES) % SSD_HEADS
    used = np.arange(LANES) < 3 * SSD_HEADS
    col_head = np.arange(SSD_D_INNER) // SSD_HEAD_DIM
    o["expand64"] = jnp.asarray((piece_head[:, None] == col_head[None, :]) & used[:, None], BF16)
    lg = np.log1p(-np.power(2.0, -5.0 - np.arange(RET_HEADS)))
    o["log_gamma"] = jnp.asarray(np.broadcast_to(lg[:, None, None], (RET_HEADS, 1, LANES)), F32)
    return o


def _rope_tables(pos0, t):
    half = RET_DK // 2
    inv = np.power(ROPE_BASE, -np.arange(half) / half)
    ang = (pos0 + np.arange(t))[:, None] * inv[None, :]
    return np.cos(ang).astype(np.float32), np.sin(ang).astype(np.float32)


def _pad_tail(buf):
    return jnp.pad(buf, ((0, 0), (SUBLANES - buf.shape[1], 0), (0, 0)))


def _tiles(nb, t):
    n = nb * t
    tm_in = min(2048, t) if t >= INPROJ_SHORT_TILE else min(INPROJ_SHORT_TILE, n)
    tm_merge = min(512, n)
    tm_ffn = min(512, t) if t >= 128 else min(128, n)
    tv = min(t, MIXER_CHUNK)
    return tm_in, tm_merge, tm_ffn, tv, max(tv, MIXER_MIN_ROWS)


def _group(x, p, pos0, states, prm):
    nb, t, _ = x.shape
    n = nb * t
    x2d = x.reshape(n, D_MODEL)
    p2d = p.reshape(n, D_PLE)
    tm_in, tm_merge, tm_ffn, Tv, L = _tiles(nb, t)
    if states is None:
        conv0 = ssd_s = ret_s = ffn0 = None
    else:
        conv_buf, ssd_s, ret_s, ffn_buf = states
        conv0, ffn0 = _pad_tail(conv_buf), _pad_tail(ffn_buf)

    cos, sin = _rope_tables(pos0, t)
    if t < tm_in:
        cos, sin = np.tile(cos, (tm_in // t, 1)), np.tile(sin, (tm_in // t, 1))
    u, dt_raw, conv_tail = _in_proj(x2d, prm, jnp.asarray(cos), jnp.asarray(sin), conv0, nb, t, tm_in)
    conv_new = conv_tail[:, SUBLANES - (SSD_CONV - 1):, :]

    y_ssd, ssd_new = _ssd(u, dt_raw, prm, ssd_s, nb, t, L, Tv)
    y_ret, ret_new = _ret(u, prm["log_gamma"], prm["ret_norm_g"], ret_s, nb, t, L, Tv)
    x1 = _merge(x2d, y_ssd, y_ret, u, prm["w_br_ssd"], prm["w_br_ret"], prm["w_out"], tm_merge)
    y, ffn_tail = _ffn(x1, p2d, prm, ffn0, nb, t, tm_ffn)
    ffn_new = ffn_tail[:, SUBLANES - (FFN_CONV - 1):, :]
    return y.reshape(nb, t, D_MODEL), conv_new, ssd_new, ret_new, ffn_new


def kernel(x_prompt, x_sample, p_prompt, p_sample, state_ssd_conv, state_ssd, state_ret, state_ffn_conv,
           norm1_g, w_in, ssd_conv_w, ssd_conv_b, dt_bias, a_log, d_skip, ssd_norm_g, w_br_ssd,
           ret_norm_g, w_br_ret, gate_b, w_out, norm2_g, w_up, ffn_conv_w, ffn_conv_b, w_down,
           ple_norm_g, w_ple_gate, w_ple_proj, final_norm_g):
    assert norm1_g.shape[0] == 1, "single-layer model"
    prm = _prep_params(norm1_g[0], w_in[0], ssd_conv_w[0], ssd_conv_b[0], dt_bias[0], a_log[0], d_skip[0],
                       ssd_norm_g[0], w_br_ssd[0], ret_norm_g[0], w_br_ret[0], gate_b[0], w_out[0],
                       norm2_g[0], w_up[0], ffn_conv_w[0], ffn_conv_b[0], w_down[0], ple_norm_g[0],
                       w_ple_gate[0], w_ple_proj[0], final_norm_g)
    yp, cp, sp, rp, fp = _group(x_prompt, p_prompt[0], 0, None, prm)
    ys, cs, ss, rs, fs = _group(x_sample, p_sample[0], PAST_LEN,
                                (state_ssd_conv[0], state_ssd[0], state_ret[0], state_ffn_conv[0]), prm)
    return (yp, ys, cp[None], sp[None], rp[None], fp[None], cs[None], ss[None], rs[None], fs[None])
```

```python
import functools

import jax
import jax.numpy as jnp
import numpy as np
from jax import lax
from jax.experimental import pallas as pl
from jax.experimental.pallas import tpu as pltpu

F32 = jnp.float32
BF16 = jnp.bfloat16

EPS = 1e-6
D_MODEL = 1024
D_PLE = 256
SSD_D_INNER = 2048
SSD_HEAD_DIM = 64
SSD_HEADS = 32
SSD_GROUPS = 4
SSD_STATE = 128
SSD_CONV = 4
SSD_BC = SSD_GROUPS * SSD_STATE
SSD_GROUP_CH = SSD_D_INNER // SSD_GROUPS
HEADS_PER_GROUP = SSD_HEADS // SSD_GROUPS
RET_HEADS = 4
RET_DK = 256
RET_DV = 512
RET_QK = RET_HEADS * RET_DK
RET_V = RET_HEADS * RET_DV
ROPE_BASE = 10000.0
D_FF = 2816
FFN_CONV = 3
PAST_LEN = 1024

U_TILE = 1024
U_COLS = 13 * U_TILE
COL_Z, COL_X, COL_V, COL_G, COL_GATES = 0, 2048, 4096, 6144, 8192
COL_Q, COL_K, COL_B, COL_C = 10240, 11264, 12288, 12800
TILE_V, TILE_Q, TILE_K, TILE_BC = COL_V // U_TILE, COL_Q // U_TILE, COL_K // U_TILE, COL_B // U_TILE
DT_PAD = 128

LANES = 128
SUBLANES = 8
MIXER_CHUNK = 256
MIXER_MIN_ROWS = 128
VMEM_LIMIT = 56 * 1024 * 1024
LOG2E = 1.4426950408889634


def _cparams(sem):
    return pltpu.CompilerParams(dimension_semantics=sem, vmem_limit_bytes=VMEM_LIMIT)


def _resident(shape):
    nd = len(shape)
    return pl.BlockSpec(shape, lambda *_: (0,) * nd, pipeline_mode=pl.Buffered(1))


def _rms(x, g):
    return x * lax.rsqrt(jnp.mean(x * x, axis=-1, keepdims=True) + EPS) * g


def _sigmoid(x):
    return 0.5 + 0.5 * jnp.tanh(0.5 * x)


def _silu(x):
    h = 0.5 * x
    return h + h * jnp.tanh(h)


def _split3(x):
    hi = x.astype(BF16)
    r1 = x - hi.astype(F32)
    mid = r1.astype(BF16)
    lo = (r1 - mid.astype(F32)).astype(BF16)
    return hi, mid, lo


def _dot(a, b):
    return jnp.dot(a, b, preferred_element_type=F32)


def _dot_nt(a, b):
    return lax.dot_general(a, b, (((1,), (1,)), ((), ())), preferred_element_type=F32)


def _dot_tn(a, b):
    return lax.dot_general(a, b, (((0,), (0,)), ((), ())), preferred_element_type=F32)


def _causal_conv(pre, prev8s, w, b):
    width = w.shape[0]
    rows, ch = pre.shape
    tiles_per_seg = rows // len(prev8s) // SUBLANES
    sub = lax.broadcasted_iota(jnp.int32, (1, SUBLANES, 1), 1)
    strips = []
    for c0 in range(0, ch, LANES):
        cs = slice(c0, c0 + LANES)
        x3 = pre[:, cs].reshape(rows // SUBLANES, SUBLANES, LANES)
        acc = b[:, cs].reshape(1, 1, LANES) + w[width - 1:width, cs].reshape(1, 1, LANES) * x3
        for j in range(1, width):
            rot = pltpu.roll(x3, j, axis=1)
            pieces = []
            for s, prev8 in enumerate(prev8s):
                pieces.append(pltpu.roll(prev8[:, cs].reshape(1, SUBLANES, LANES), j, axis=1))
                if tiles_per_seg > 1:
                    pieces.append(rot[s * tiles_per_seg:(s + 1) * tiles_per_seg - 1])
            prev = pieces[0] if len(pieces) == 1 else jnp.concatenate(pieces, axis=0)
            acc = acc + w[width - 1 - j:width - j, cs].reshape(1, 1, LANES) * jnp.where(sub < j, prev, rot)
        strips.append(acc.reshape(rows, LANES))
    return strips[0] if len(strips) == 1 else jnp.concatenate(strips, axis=1)


INPROJ_SUB = 256
INPROJ_CONV_SUB = 512
INPROJ_SHORT_TILE = 512


def _inproj_kernel(*refs, tm, seg, nt, has_c0):
    if has_c0:
        (x_ref, g_ref, wh_ref, wt_ref, wdt_ref, gb_ref, cos_ref, sin_ref, cw_ref, cbias_ref, c0_ref,
         u_ref, dt_ref, co_ref, h_ref, carry_scr, acc_scr) = refs
    else:
        (x_ref, g_ref, wh_ref, wt_ref, wdt_ref, gb_ref, cos_ref, sin_ref, cw_ref, cbias_ref,
         u_ref, dt_ref, co_ref, h_ref, carry_scr, acc_scr) = refs
        c0_ref = None
    i = pl.program_id(0)
    j = pl.program_id(1)
    sub = min(INPROJ_SUB, tm)
    nseg = tm // seg

    @pl.when(j == 0)
    def _():
        for r in range(tm // sub):
            rs = slice(r * sub, (r + 1) * sub)
            hb = _rms(x_ref[rs, :], g_ref[...]).astype(BF16)
            h_ref[rs, :] = hb
            dt_ref[rs, :] = _dot(hb, wdt_ref[...])

    def tiles(epilogue, w_ref):
        w = w_ref[...]
        for r in range(tm // sub):
            rs = slice(r * sub, (r + 1) * sub)
            u_ref[rs, :] = epilogue(_dot(h_ref[rs, :], w), rs).astype(BF16)

    def rotary(scale):
        def ep(acc, rs):
            cos = cos_ref[rs, :]
            sin = sin_ref[rs, :]
            half = RET_DK // 2
            out = []
            for h in range(U_TILE // RET_DK):
                x1 = acc[:, h * RET_DK:h * RET_DK + half]
                x2 = acc[:, h * RET_DK + half:(h + 1) * RET_DK]
                out += [(x1 * cos - x2 * sin) * scale, (x2 * cos + x1 * sin) * scale]
            return jnp.concatenate(out, axis=1)
        return ep

    is_silu = (j < COL_X // U_TILE) | ((j >= COL_G // U_TILE) & (j < COL_GATES // U_TILE))
    is_gate = (j >= COL_GATES // U_TILE) & (j < TILE_Q)
    is_conv = ((j >= COL_X // U_TILE) & (j < COL_V // U_TILE)) | (j == TILE_BC)
    is_plain = jnp.logical_not(is_silu | is_gate | is_conv | (j == TILE_Q) | (j == TILE_K))

    @pl.when(is_plain)
    def _():
        tiles(lambda acc, rs: acc, wt_ref)

    @pl.when(is_conv)
    def _():
        cblk = _conv_block(j)
        cw = cw_ref[...]
        cbias = cbias_ref[...]
        wh = wh_ref[...]
        if nseg == 1:
            carried = carry_scr[cblk]
            init = c0_ref[0] if has_c0 else jnp.zeros_like(carried)
            prev = jnp.where(i % nt == 0, init, carried)
            csub = min(INPROJ_CONV_SUB, tm)
            for r in range(tm // csub):
                rs = slice(r * csub, (r + 1) * csub)
                slot = (j + r) % 2
                acc_scr[slot] = _dot(h_ref[rs, :], wh)
                acc = acc_scr[slot]
                u_ref[rs, :] = _silu(_causal_conv(acc, [prev], cw, cbias)).astype(BF16)
                prev = acc[csub - SUBLANES:csub]
            carry_scr[cblk] = prev
            co_ref[0] = prev
        else:
            acc = _dot(h_ref[...], wh)
            zero8 = jnp.zeros((SUBLANES, U_TILE), F32)
            prev8s = [c0_ref[s] if has_c0 else zero8 for s in range(nseg)]
            u_ref[...] = _silu(_causal_conv(acc, prev8s, cw, cbias)).astype(BF16)
            for s in range(nseg):
                co_ref[s] = acc[(s + 1) * seg - SUBLANES:(s + 1) * seg]

    @pl.when(is_silu & (j < TILE_V))
    def _():
        tiles(lambda acc, rs: _silu(acc), wh_ref)

    @pl.when(is_silu & (j >= TILE_V))
    def _():
        tiles(lambda acc, rs: _silu(acc), wt_ref)

    @pl.when(is_gate)
    def _():
        tiles(lambda acc, rs: _sigmoid(acc + gb_ref[...]), wt_ref)

    @pl.when(j == TILE_Q)
    def _():
        tiles(rotary(1.0), wt_ref)

    @pl.when(j == TILE_K)
    def _():
        tiles(rotary(RET_DK ** -0.5), wt_ref)


def _conv_block(j):
    return jnp.where(j >= TILE_BC, 2, jnp.where(j > COL_X // U_TILE, 1, 0))


def _in_proj(x2d, prm, cos, sin, conv0, nb, t, tm):
    n = x2d.shape[0]
    npos = cos.shape[0] // tm
    seg = min(t, tm)
    nseg = tm // seg
    nt = t // seg
    assert nseg == 1 or tm <= INPROJ_SHORT_TILE
    gate_tile0 = COL_GATES // U_TILE
    gb_map = lambda i, j: (0, jnp.clip(j - gate_tile0, 0, 2 * D_MODEL // U_TILE - 1))
    cmap = lambda i, j: (0, _conv_block(j))
    tail_spec = pl.BlockSpec((nseg, SUBLANES, U_TILE), lambda i, j: (i // nt, 0, _conv_block(j)))
    operands = [x2d, prm["norm1_g"], prm["w_head"], prm["w_tail"], prm["w_dt"], prm["gate_b"], cos, sin,
                prm["ssd_conv_w"], prm["ssd_conv_b"]]
    in_specs = [
        pl.BlockSpec((tm, D_MODEL), lambda i, j: (i, 0)),
        pl.BlockSpec((1, D_MODEL), lambda i, j: (0, 0)),
        pl.BlockSpec((D_MODEL, U_TILE), lambda i, j: (0, jnp.minimum(j, TILE_V))),
        pl.BlockSpec((D_MODEL, U_TILE), lambda i, j: (0, jnp.where(
            j < TILE_V, 2, jnp.where(j < TILE_Q, j - 2, jnp.where(j == TILE_Q, 0, 1))))),
        pl.BlockSpec((D_MODEL, DT_PAD), lambda i, j: (0, 0)),
        pl.BlockSpec((1, U_TILE), gb_map),
        pl.BlockSpec((tm, RET_DK // 2), lambda i, j: (i % npos, 0)),
        pl.BlockSpec((tm, RET_DK // 2), lambda i, j: (i % npos, 0)),
        pl.BlockSpec((SSD_CONV, U_TILE), cmap),
        pl.BlockSpec((1, U_TILE), cmap),
    ]
    if conv0 is not None:
        operands.append(conv0)
        in_specs.append(tail_spec)
    u, dt_raw, tails = pl.pallas_call(
        functools.partial(_inproj_kernel, tm=tm, seg=seg, nt=nt, has_c0=conv0 is not None),
        grid=(n // tm, U_COLS // U_TILE),
        in_specs=in_specs,
        out_specs=[
            pl.BlockSpec((tm, U_TILE), lambda i, j: (i, j)),
            pl.BlockSpec((tm, DT_PAD), lambda i, j: (i, 0)),
            pl.BlockSpec((nseg, SUBLANES, U_TILE), lambda i, j: (i, 0, _conv_block(j))),
        ],
        out_shape=[
            jax.ShapeDtypeStruct((n, U_COLS), BF16),
            jax.ShapeDtypeStruct((n, DT_PAD), F32),
            jax.ShapeDtypeStruct((nb * nt, SUBLANES, SSD_D_INNER + 2 * SSD_BC), F32),
        ],
        scratch_shapes=[pltpu.VMEM((tm, D_MODEL), BF16), pltpu.VMEM((3, SUBLANES, U_TILE), F32),
                        pltpu.VMEM((2, min(INPROJ_CONV_SUB, tm), U_TILE), F32)],
        compiler_params=_cparams(("arbitrary", "arbitrary")),
        name="in_proj",
    )(*operands)
    xbc = SSD_D_INNER + 2 * SSD_BC
    return u, dt_raw, tails.reshape(nb, nt, SUBLANES, xbc)[:, nt - 1]


def _ssd_kernel(*refs, L, Tv, has_s0):
    if has_s0:
        (x_ref, b_ref, c_ref, z_ref, dt_ref, dtb_ref, alog_ref, dsk_ref, ng_ref, e64_ref, s0_ref,
         y_ref, so_ref, s_scr, *pad_scr) = refs
    else:
        (x_ref, b_ref, c_ref, z_ref, dt_ref, dtb_ref, alog_ref, dsk_ref, ng_ref, e64_ref,
         y_ref, so_ref, s_scr, *pad_scr) = refs
    c = pl.program_id(1)
    nc = pl.num_programs(1)
    Q = LANES
    nq = L // Q

    @pl.when(c == 0)
    def _():
        if has_s0:
            s_scr[...] = s0_ref[0].reshape(SSD_D_INNER, SSD_STATE)
        else:
            s_scr[...] = jnp.zeros(s_scr.shape, F32)

    if Tv < L:
        for scr, ref in zip(pad_scr, (x_ref, b_ref, c_ref, z_ref, dt_ref)):
            scr[...] = jnp.zeros(scr.shape, scr.dtype)
            scr[0:Tv, :] = ref[...]
        x_in, b_in, c_in, z_in, dt_in = pad_scr
    else:
        x_in, b_in, c_in, z_in, dt_in = x_ref, b_ref, c_ref, z_ref, dt_ref

    rows = lax.broadcasted_iota(jnp.int32, (L, 1), 0)
    dtv = dt_in[...] + dtb_ref[...]
    dt = jnp.maximum(dtv, 0.0) + jnp.log(1.0 + jnp.exp(-jnp.abs(dtv)))
    if Tv < L:
        dt = jnp.where(rows < Tv, dt, 0.0)
    dA = dt * (-jnp.exp(alog_ref[...]))

    causal = (lax.broadcasted_iota(jnp.int32, (L, L), 0) >= lax.broadcasted_iota(jnp.int32, (L, L), 1))
    tri = jnp.where(causal, 1.0, 0.0).astype(BF16)
    hi, mid, lo = _split3(dA)
    cum = _dot(tri, hi) + _dot(tri, mid) + _dot(tri, lo)
    dec_tot = jnp.exp(cum[Tv - 1:Tv, :])
    c2 = cum * LOG2E
    r_t = (c2 - jnp.log2(dt)).T
    tri_q = (lax.broadcasted_iota(jnp.int32, (Q, Q), 0) >= lax.broadcasted_iota(jnp.int32, (Q, Q), 1))

    lane = lax.broadcasted_iota(jnp.int32, (1, LANES), 1)

    def pack3(v):
        r1 = v - v.astype(BF16).astype(F32)
        r2 = r1 - r1.astype(BF16).astype(F32)
        return jnp.where(lane < SSD_HEADS, v,
                         jnp.where(lane < 2 * SSD_HEADS, pltpu.roll(r1, SSD_HEADS, axis=1),
                                   pltpu.roll(r2, 2 * SSD_HEADS, axis=1))).astype(BF16)

    c2p = pack3(c2)
    dtp = pack3(dt)

    lo_half = lane < SSD_HEAD_DIM
    P2 = 2 * SSD_HEAD_DIM

    for g in range(SSD_GROUPS):
        gx = slice(g * SSD_GROUP_CH, (g + 1) * SSD_GROUP_CH)
        gn = slice(g * SSD_STATE, (g + 1) * SSD_STATE)
        c2_x64 = _dot(c2p, e64_ref[:, gx])
        dt_x64 = _dot(dtp, e64_ref[:, gx])
        ecum_x = jnp.exp2(c2_x64)
        todt_x = jnp.exp2(c2_x64[Tv - 1:Tv, :] - c2_x64) * dt_x64
        xcb = x_in[:, gx]
        bcb = b_in[:, gn]
        ccb = c_in[:, gn]
        xc = xcb.astype(F32)
        cb = _dot_nt(ccb, bcb)
        s_old = s_scr[gx, :]
        y_inter = _dot_nt(ccb, s_old.astype(BF16))

        y_parts = []
        xw_parts = []
        for pr in range(HEADS_PER_GROUP // 2):
            heads = [g * HEADS_PER_GROUP + 2 * pr + k for k in range(2)]
            ps = slice(pr * P2, (pr + 1) * P2)
            x_pair = xc[:, ps]
            x_pair_b = xcb[:, ps]
            zero = jnp.zeros_like(x_pair_b)
            x_a = jnp.where(lo_half, x_pair_b, zero)
            x_b = jnp.where(lo_half, zero, x_pair_b)
            y_rows = []
            for i in range(nq):
                ri = slice(i * Q, (i + 1) * Q)
                lhs, rhs = [], []
                for h, xh in zip(heads, (x_a, x_b)):
                    for jq in range(i + 1):
                        rj = slice(jq * Q, (jq + 1) * Q)
                        e = jnp.exp2(c2[ri, h:h + 1] - r_t[h:h + 1, rj])
                        if jq == i:
                            e = jnp.where(tri_q, e, 0.0)
                        lhs.append((e * cb[ri, rj]).astype(BF16))
                    rhs.append(xh[0:(i + 1) * Q])
                y_rows.append(_dot(jnp.concatenate(lhs, axis=1), jnp.concatenate(rhs, axis=0)))
            y_pair = y_rows[0] if nq == 1 else jnp.concatenate(y_rows, axis=0)
            pg = slice(g * SSD_GROUP_CH + pr * P2, g * SSD_GROUP_CH + (pr + 1) * P2)
            y_pair = y_pair + y_inter[:, ps] * ecum_x[:, ps]
            y_pair = y_pair + x_pair * dsk_ref[:, pg]
            y_parts.append(y_pair)
            xw_parts.append((x_pair * todt_x[:, ps]).astype(BF16))

        y = jnp.concatenate(y_parts, axis=1)
        xw = jnp.concatenate(xw_parts, axis=1)

        dec_rows = [jnp.broadcast_to(dec_tot[:, g * HEADS_PER_GROUP + k:g * HEADS_PER_GROUP + k + 1],
                                     (SSD_HEAD_DIM, 1)) for k in range(HEADS_PER_GROUP)]
        s_new = s_old * jnp.concatenate(dec_rows, axis=0) + _dot_tn(xw, bcb)
        s_scr[gx, :] = s_new

        yz = y * z_in[:, gx].astype(F32)
        yn = yz * lax.rsqrt(jnp.mean(yz * yz, axis=-1, keepdims=True) + EPS) * ng_ref[:, gx]
        y_ref[:, gx] = yn[0:Tv].astype(BF16)

    @pl.when(c == nc - 1)
    def _():
        so_ref[0] = s_scr[...].reshape(SSD_HEADS, SSD_HEAD_DIM, SSD_STATE)


def _ssd(u, dt_raw, prm, s0, nb, t, L, Tv):
    nc = t // Tv
    rb = lambda b, c: b * nc + c
    full = lambda w: pl.BlockSpec(w.shape, lambda b, c: (0,) * w.ndim)
    state_spec = pl.BlockSpec((1, SSD_HEADS, SSD_HEAD_DIM, SSD_STATE), lambda b, c: (b, 0, 0, 0))
    names = ["dt_bias", "a_log", "d_skip", "ssd_norm_g", "expand64"]
    operands = [u, u, u, u, dt_raw] + [prm[k] for k in names]
    in_specs = [
        pl.BlockSpec((Tv, SSD_D_INNER), lambda b, c: (rb(b, c), COL_X // SSD_D_INNER)),
        pl.BlockSpec((Tv, SSD_BC), lambda b, c: (rb(b, c), COL_B // SSD_BC)),
        pl.BlockSpec((Tv, SSD_BC), lambda b, c: (rb(b, c), COL_C // SSD_BC)),
        pl.BlockSpec((Tv, SSD_D_INNER), lambda b, c: (rb(b, c), COL_Z // SSD_D_INNER)),
        pl.BlockSpec((Tv, DT_PAD), lambda b, c: (rb(b, c), 0)),
    ] + [full(prm[k]) for k in names]
    if s0 is not None:
        operands.append(s0)
        in_specs.append(state_spec)
    scratch = [pltpu.VMEM((SSD_D_INNER, SSD_STATE), F32)]
    if Tv < L:
        scratch += [
            pltpu.VMEM((L, SSD_D_INNER), BF16), pltpu.VMEM((L, SSD_BC), BF16),
            pltpu.VMEM((L, SSD_BC), BF16), pltpu.VMEM((L, SSD_D_INNER), BF16),
            pltpu.VMEM((L, DT_PAD), F32),
        ]
    return pl.pallas_call(
        functools.partial(_ssd_kernel, L=L, Tv=Tv, has_s0=s0 is not None),
        grid=(nb, nc),
        in_specs=in_specs,
        out_specs=[pl.BlockSpec((Tv, SSD_D_INNER), lambda b, c: (rb(b, c), 0)), state_spec],
        out_shape=[jax.ShapeDtypeStruct((nb * t, SSD_D_INNER), BF16),
                   jax.ShapeDtypeStruct((nb, SSD_HEADS, SSD_HEAD_DIM, SSD_STATE), F32)],
        scratch_shapes=scratch,
        compiler_params=_cparams(("parallel", "arbitrary")),
        name="ssd",
    )(*operands)


def _ret_kernel(*refs, L, Tv, has_s0):
    if has_s0:
        (q_ref, k_ref, v_ref, g_ref, lg_ref, ng_ref, s0_ref,
         y_ref, so_ref, s_scr, dm_scr, cross_scr, kdec_scr, *pad_scr) = refs
    else:
        (q_ref, k_ref, v_ref, g_ref, lg_ref, ng_ref,
         y_ref, so_ref, s_scr, dm_scr, cross_scr, kdec_scr, *pad_scr) = refs
    c = pl.program_id(1)
    nc = pl.num_programs(1)

    @pl.when(c == 0)
    def _():
        if has_s0:
            s_scr[...] = s0_ref[0]
        else:
            s_scr[...] = jnp.zeros(s_scr.shape, F32)
        rowf = lax.broadcasted_iota(jnp.int32, (L, 1), 0).astype(F32)
        colf = lax.broadcasted_iota(jnp.int32, (1, L), 1).astype(F32)
        diff = rowf - colf
        for h in range(RET_HEADS):
            lg = lg_ref[h][:, 0:1]
            dm_scr[h] = jnp.where(diff >= 0.0, jnp.exp(jnp.maximum(diff, 0.0) * lg), 0.0)
            cross_scr[h] = jnp.broadcast_to(jnp.exp((rowf + 1.0) * lg), (L, LANES))
            kdec = jnp.exp((Tv - 1.0 - rowf) * lg)
            if Tv < L:
                kdec = jnp.where(rowf < Tv, kdec, 0.0)
            kdec_scr[h] = jnp.broadcast_to(kdec, (L, LANES))

    if Tv < L:
        for scr, ref in zip(pad_scr, (q_ref, k_ref, v_ref, g_ref)):
            scr[...] = jnp.zeros(scr.shape, scr.dtype)
            scr[0:Tv, :] = ref[...]
        q_in, k_in, v_in, g_in = pad_scr
    else:
        q_in, k_in, v_in, g_in = q_ref, k_ref, v_ref, g_ref

    for h in range(RET_HEADS):
        ks = slice(h * RET_DK, (h + 1) * RET_DK)
        vs = slice(h * RET_DV, (h + 1) * RET_DV)
        qh = q_in[:, ks]
        kh = k_in[:, ks]
        vh = v_in[:, vs]
        s_old = s_scr[h]
        sc = (_dot_nt(qh, kh) * dm_scr[h]).astype(BF16)
        cross = jnp.concatenate([cross_scr[h]] * (RET_DV // LANES), axis=1)
        o = _dot(sc, vh) + _dot(qh, s_old.astype(BF16)) * cross
        kdec = jnp.concatenate([kdec_scr[h]] * (RET_DK // LANES), axis=1)
        kd = (kh.astype(F32) * kdec).astype(BF16)
        s_new = s_old * jnp.exp(Tv * lg_ref[h][:, 0:1]) + _dot_tn(kd, vh)
        s_scr[h] = s_new

        mu = jnp.mean(o, axis=-1, keepdims=True)
        d = o - mu
        var = jnp.mean(d * d, axis=-1, keepdims=True)
        y = d * lax.rsqrt(var + EPS) * ng_ref[:, vs] * g_in[:, vs].astype(F32)
        y_ref[:, vs] = y[0:Tv].astype(BF16)

    @pl.when(c == nc - 1)
    def _():
        so_ref[0] = s_scr[...]


def _ret(u, lg, ng, s0, nb, t, L, Tv):
    nc = t // Tv
    rb = lambda b, c: b * nc + c
    state_spec = pl.BlockSpec((1, RET_HEADS, RET_DK, RET_DV), lambda b, c: (b, 0, 0, 0))
    in_specs = [
        pl.BlockSpec((Tv, RET_QK), lambda b, c: (rb(b, c), COL_Q // RET_QK)),
        pl.BlockSpec((Tv, RET_QK), lambda b, c: (rb(b, c), COL_K // RET_QK)),
        pl.BlockSpec((Tv, RET_V), lambda b, c: (rb(b, c), COL_V // RET_V)),
        pl.BlockSpec((Tv, RET_V), lambda b, c: (rb(b, c), COL_G // RET_V)),
        pl.BlockSpec(lg.shape, lambda b, c: (0, 0, 0)),
        pl.BlockSpec((1, RET_V), lambda b, c: (0, 0)),
    ]
    operands = [u, u, u, u, lg, ng]
    if s0 is not None:
        operands.append(s0)
        in_specs.append(state_spec)
    out_specs = [pl.BlockSpec((Tv, RET_V), lambda b, c: (rb(b, c), 0)), state_spec]
    out_shape = [
        jax.ShapeDtypeStruct((nb * t, RET_V), BF16),
        jax.ShapeDtypeStruct((nb, RET_HEADS, RET_DK, RET_DV), F32),
    ]
    scratch = [
        pltpu.VMEM((RET_HEADS, RET_DK, RET_DV), F32),
        pltpu.VMEM((RET_HEADS, L, L), F32),
        pltpu.VMEM((RET_HEADS, L, LANES), F32),
        pltpu.VMEM((RET_HEADS, L, LANES), F32),
    ]
    if Tv < L:
        scratch += [
            pltpu.VMEM((L, RET_QK), BF16), pltpu.VMEM((L, RET_QK), BF16),
            pltpu.VMEM((L, RET_V), BF16), pltpu.VMEM((L, RET_V), BF16),
        ]
    return pl.pallas_call(
        functools.partial(_ret_kernel, L=L, Tv=Tv, has_s0=s0 is not None),
        grid=(nb, nc),
        in_specs=in_specs, out_specs=out_specs, out_shape=out_shape,
        scratch_shapes=scratch,
        compiler_params=_cparams(("parallel", "arbitrary")),
        name="retention",
    )(*operands)


def _merge_kernel(x_ref, ys_ref, yr_ref, gt_ref, ws_ref, wr_ref, wo_ref, o_ref):
    a = _dot(ys_ref[...], ws_ref[...])
    b = _dot(yr_ref[...], wr_ref[...])
    mix = gt_ref[:, :D_MODEL].astype(F32) * a + gt_ref[:, D_MODEL:].astype(F32) * b
    o_ref[...] = x_ref[...] + _dot(mix.astype(BF16), wo_ref[...])


def _merge(x2d, y_ssd, y_ret, u, w_s, w_r, w_o, tm):
    n = x2d.shape[0]
    return pl.pallas_call(
        _merge_kernel,
        grid=(n // tm,),
        in_specs=[
            pl.BlockSpec((tm, D_MODEL), lambda i: (i, 0)),
            pl.BlockSpec((tm, SSD_D_INNER), lambda i: (i, 0)),
            pl.BlockSpec((tm, RET_V), lambda i: (i, 0)),
            pl.BlockSpec((tm, 2 * D_MODEL), lambda i: (i, COL_GATES // (2 * D_MODEL))),
            _resident(w_s.shape), _resident(w_r.shape), _resident(w_o.shape),
        ],
        out_specs=pl.BlockSpec((tm, D_MODEL), lambda i: (i, 0)),
        out_shape=jax.ShapeDtypeStruct((n, D_MODEL), F32),
        compiler_params=_cparams(("parallel",)),
        name="merge",
    )(x2d, y_ssd, y_ret, u, w_s, w_r, w_o)


FFN_CHUNK = 256
FFN_NCHUNK = D_FF // FFN_CHUNK


def _ffn_kernel(*refs, tm, seg, has_c0):
    if has_c0:
        (x_ref, p_ref, n2_ref, wup_ref, cw_ref, cb_ref, wdn_ref, pg_ref, wpg_ref, wpp_ref, fg_ref, c0_ref,
         y_ref, co_ref, act_scr, carry_scr) = refs
    else:
        (x_ref, p_ref, n2_ref, wup_ref, cw_ref, cb_ref, wdn_ref, pg_ref, wpg_ref, wpp_ref, fg_ref,
         y_ref, co_ref, act_scr, carry_scr) = refs
    nseg = tm // seg
    if nseg == 1:
        i = pl.program_id(1)

        @pl.when(i == 0)
        def _():
            if has_c0:
                carry_scr[...] = c0_ref[...]
            else:
                carry_scr[...] = jnp.zeros(carry_scr.shape, F32)
        prev_ref = carry_scr
    else:
        assert has_c0
        prev_ref = c0_ref

    x1 = x_ref[...]
    h2 = _rms(x1, n2_ref[...]).astype(BF16)

    for cc in range(FFN_NCHUNK):
        halves = []
        for base in (0, D_FF):
            sl = slice(base + cc * FFN_CHUNK, base + (cc + 1) * FFN_CHUNK)
            up = _dot(h2, wup_ref[:, sl])
            prev8s = [prev_ref[s, :, sl] for s in range(nseg)]
            halves.append(_causal_conv(up, prev8s, cw_ref[:, sl], cb_ref[:, sl]))
            for s in range(nseg):
                tail = up[(s + 1) * seg - SUBLANES:(s + 1) * seg]
                if nseg == 1:
                    carry_scr[s, :, sl] = tail
                else:
                    co_ref[s, :, sl] = tail
        a, b = halves
        gelu = 0.5 * a * (1.0 + lax.erf(a * (2.0 ** -0.5)))
        act_scr[:, cc * FFN_CHUNK:(cc + 1) * FFN_CHUNK] = (gelu * b).astype(BF16)

    x2 = x1 + _dot(act_scr[...], wdn_ref[...])
    hg = _rms(x2, pg_ref[...]).astype(BF16)
    gate = _sigmoid(_dot(hg, wpg_ref[...]))
    x3 = x2 + gate * _dot(p_ref[...].astype(BF16), wpp_ref[...])
    y_ref[...] = _rms(x3, fg_ref[...])

    if nseg == 1:
        @pl.when(i == pl.num_programs(1) - 1)
        def _():
            co_ref[...] = carry_scr[...]


def _ffn(x1, p2d, prm, tails, nb, t, tm):
    n = nb * t
    seg = min(t, tm)
    nseg = tm // seg
    weights = [prm[k] for k in ("norm2_g", "w_up", "ffn_conv_w", "ffn_conv_b", "w_down", "ple_norm_g",
                                "w_ple_gate", "w_ple_proj", "final_norm_g")]
    wspecs = [_resident(w.shape) for w in weights]
    nt = t // seg
    grid = (n // (tm * nt), nt)
    row = lambda b, i: (b * nt + i, 0)
    tail_spec = pl.BlockSpec((nseg, SUBLANES, 2 * D_FF), lambda b, i: (b, 0, 0))
    operands = [x1, p2d, *weights]
    in_specs = [pl.BlockSpec((tm, D_MODEL), row), pl.BlockSpec((tm, D_PLE), row)] + wspecs
    if tails is not None:
        operands.append(tails)
        in_specs.append(tail_spec)
    return pl.pallas_call(
        functools.partial(_ffn_kernel, tm=tm, seg=seg, has_c0=tails is not None),
        grid=grid,
        in_specs=in_specs,
        out_specs=[pl.BlockSpec((tm, D_MODEL), row), tail_spec],
        out_shape=[jax.ShapeDtypeStruct((n, D_MODEL), F32),
                   jax.ShapeDtypeStruct((nb, SUBLANES, 2 * D_FF), F32)],
        scratch_shapes=[pltpu.VMEM((tm, D_FF), BF16), pltpu.VMEM((1, SUBLANES, 2 * D_FF), F32)],
        compiler_params=_cparams(("parallel", "arbitrary")),
        name="ffn",
    )(*operands)


def _prep_params(norm1_g, w_in, ssd_conv_w, ssd_conv_b, dt_bias, a_log, d_skip, ssd_norm_g, w_br_ssd,
                 ret_norm_g, w_br_ret, gate_b, w_out, norm2_g, w_up, ffn_conv_w, ffn_conv_b, w_down,
                 ple_norm_g, w_ple_gate, w_ple_proj, final_norm_g):
    o = {}
    n_head = 2 * SSD_D_INNER + 2 * SSD_BC
    o["w_head"] = w_in[:, :n_head].astype(BF16)
    o["w_tail"] = w_in[:, n_head + SSD_HEADS:].astype(BF16)
    o["w_dt"] = jnp.pad(w_in[:, n_head:n_head + SSD_HEADS], ((0, 0), (0, DT_PAD - SSD_HEADS))).astype(BF16)
    o["norm1_g"] = norm1_g.reshape(1, -1)
    o["ssd_conv_w"] = ssd_conv_w
    o["ssd_conv_b"] = ssd_conv_b.reshape(1, -1)
    o["dt_bias"] = jnp.pad(dt_bias.reshape(1, -1), ((0, 0), (0, DT_PAD - SSD_HEADS)))
    o["a_log"] = jnp.pad(a_log.reshape(1, -1), ((0, 0), (0, DT_PAD - SSD_HEADS)))
    o["d_skip"] = jnp.repeat(d_skip, SSD_HEAD_DIM).reshape(1, -1)
    o["ssd_norm_g"] = ssd_norm_g.reshape(1, -1)
    o["w_br_ssd"] = w_br_ssd.astype(BF16)
    o["ret_norm_g"] = ret_norm_g.reshape(1, -1)
    o["w_br_ret"] = w_br_ret.astype(BF16)
    o["gate_b"] = gate_b.reshape(1, -1)
    o["w_out"] = w_out.astype(BF16)
    o["norm2_g"] = norm2_g.reshape(1, -1)
    o["w_up"] = w_up.astype(BF16)
    o["ffn_conv_w"] = ffn_conv_w
    o["ffn_conv_b"] = ffn_conv_b.reshape(1, -1)
    o["w_down"] = w_down.astype(BF16)
    o["ple_norm_g"] = ple_norm_g.reshape(1, -1)
    o["w_ple_gate"] = w_ple_gate.astype(BF16)
    o["w_ple_proj"] = w_ple_proj.astype(BF16)
    o["final_norm_g"] = final_norm_g.reshape(1, -1)
    piece_head = np.arange(LANES) % SSD_HEADS
    used = np.arange(LANES) < 3 * SSD_HEADS
    col_head = np.arange(SSD_D_INNER) // SSD_HEAD_DIM
    o["expand64"] = jnp.asarray((piece_head[:, None] == col_head[None, :]) & used[:, None], BF16)
    lg = np.log1p(-np.power(2.0, -5.0 - np.arange(RET_HEADS)))
    o["log_gamma"] = jnp.asarray(np.broadcast_to(lg[:, None, None], (RET_HEADS, 1, LANES)), F32)
    return o


def _rope_tables(pos0, t):
    half = RET_DK // 2
    inv = np.power(ROPE_BASE, -np.arange(half) / half)
    ang = (pos0 + np.arange(t))[:, None] * inv[None, :]
    return np.cos(ang).astype(np.float32), np.sin(ang).astype(np.float32)


def _pad_tail(buf):
    return jnp.pad(buf, ((0, 0), (SUBLANES - buf.shape[1], 0), (0, 0)))


def _tiles(nb, t):
    n = nb * t
    tm_in = min(2048, t) if t >= INPROJ_SHORT_TILE else min(INPROJ_SHORT_TILE, n)
    tm_merge = min(512, n)
    tm_ffn = min(512, t) if t >= 128 else min(128, n)
    tv = min(t, MIXER_CHUNK)
    return tm_in, tm_merge, tm_ffn, tv, max(tv, MIXER_MIN_ROWS)


def _group(x, p, pos0, states, prm):
    nb, t, _ = x.shape
    n = nb * t
    x2d = x.reshape(n, D_MODEL)
    p2d = p.reshape(n, D_PLE)
    tm_in, tm_merge, tm_ffn, Tv, L = _tiles(nb, t)
    if states is None:
        conv0 = ssd_s = ret_s = ffn0 = None
    else:
        conv_buf, ssd_s, ret_s, ffn_buf = states
        conv0, ffn0 = _pad_tail(conv_buf), _pad_tail(ffn_buf)

    cos, sin = _rope_tables(pos0, t)
    if t < tm_in:
        cos, sin = np.tile(cos, (tm_in // t, 1)), np.tile(sin, (tm_in // t, 1))
    u, dt_raw, conv_tail = _in_proj(x2d, prm, jnp.asarray(cos), jnp.asarray(sin), conv0, nb, t, tm_in)
    conv_new = conv_tail[:, SUBLANES - (SSD_CONV - 1):, :]

    y_ssd, ssd_new = _ssd(u, dt_raw, prm, ssd_s, nb, t, L, Tv)
    y_ret, ret_new = _ret(u, prm["log_gamma"], prm["ret_norm_g"], ret_s, nb, t, L, Tv)
    x1 = _merge(x2d, y_ssd, y_ret, u, prm["w_br_ssd"], prm["w_br_ret"], prm["w_out"], tm_merge)
    y, ffn_tail = _ffn(x1, p2d, prm, ffn0, nb, t, tm_ffn)
    ffn_new = ffn_tail[:, SUBLANES - (FFN_CONV - 1):, :]
    return y.reshape(nb, t, D_MODEL), conv_new, ssd_new, ret_new, ffn_new


def kernel(x_prompt, x_sample, p_prompt, p_sample, state_ssd_conv, state_ssd, state_ret, state_ffn_conv,
           norm1_g, w_in, ssd_conv_w, ssd_conv_b, dt_bias, a_log, d_skip, ssd_norm_g, w_br_ssd,
           ret_norm_g, w_br_ret, gate_b, w_out, norm2_g, w_up, ffn_conv_w, ffn_conv_b, w_down,
           ple_norm_g, w_ple_gate, w_ple_proj, final_norm_g):
    assert norm1_g.shape[0] == 1, "single-layer model"
    prm = _prep_params(norm1_g[0], w_in[0], ssd_conv_w[0], ssd_conv_b[0], dt_bias[0], a_log[0], d_skip[0],
                       ssd_norm_g[0], w_br_ssd[0], ret_norm_g[0], w_br_ret[0], gate_b[0], w_out[0],
                       norm2_g[0], w_up[0], ffn_conv_w[0], ffn_conv_b[0], w_down[0], ple_norm_g[0],
                       w_ple_gate[0], w_ple_proj[0], final_norm_g)
    yp, cp, sp, rp, fp = _group(x_prompt, p_prompt[0], 0, None, prm)
    ys, cs, ss, rs, fs = _group(x_sample, p_sample[0], PAST_LEN,
                                (state_ssd_conv[0], state_ssd[0], state_ret[0], state_ffn_conv[0]), prm)
    return (yp, ys, cp[None], sp[None], rp[None], fp[None], cs[None], ss[None], rs[None], fs[None])
```

```python
import functools

import jax
import jax.numpy as jnp
import numpy as np
from jax import lax
from jax.experimental import pallas as pl
from jax.experimental.pallas import tpu as pltpu

F32 = jnp.float32
BF16 = jnp.bfloat16

EPS = 1e-6
D_MODEL = 1024
D_PLE = 256
SSD_D_INNER = 2048
SSD_HEAD_DIM = 64
SSD_HEADS = 32
SSD_GROUPS = 4
SSD_STATE = 128
SSD_CONV = 4
SSD_BC = SSD_GROUPS * SSD_STATE
SSD_GROUP_CH = SSD_D_INNER // SSD_GROUPS
HEADS_PER_GROUP = SSD_HEADS // SSD_GROUPS
RET_HEADS = 4
RET_DK = 256
RET_DV = 512
RET_QK = RET_HEADS * RET_DK
RET_V = RET_HEADS * RET_DV
ROPE_BASE = 10000.0
D_FF = 2816
FFN_CONV = 3
PAST_LEN = 1024

U_TILE = 1024
U_COLS = 13 * U_TILE
COL_Z, COL_X, COL_V, COL_G, COL_GATES = 0, 2048, 4096, 6144, 8192
COL_Q, COL_K, COL_B, COL_C = 10240, 11264, 12288, 12800
TILE_V, TILE_Q, TILE_K, TILE_BC = COL_V // U_TILE, COL_Q // U_TILE, COL_K // U_TILE, COL_B // U_TILE
DT_PAD = 128

LANES = 128
SUBLANES = 8
INPROJ_TILE = 2048
MERGE_TILE = 512
FFN_TILE = 512
FFN_SHORT_TILE = 128
MIXER_CHUNK = 256
MIXER_MIN_ROWS = 128
VMEM_LIMIT = 56 * 1024 * 1024
LOG2E = 1.4426950408889634


def _cparams(sem):
    return pltpu.CompilerParams(dimension_semantics=sem, vmem_limit_bytes=VMEM_LIMIT)


def _resident(shape):
    nd = len(shape)
    return pl.BlockSpec(shape, lambda *_: (0,) * nd, pipeline_mode=pl.Buffered(1))


def _rms(x, g):
    return x * lax.rsqrt(jnp.mean(x * x, axis=-1, keepdims=True) + EPS) * g


def _sigmoid(x):
    return 0.5 + 0.5 * jnp.tanh(0.5 * x)


def _silu(x):
    h = 0.5 * x
    return h + h * jnp.tanh(h)


def _split3(x):
    hi = x.astype(BF16)
    r1 = x - hi.astype(F32)
    mid = r1.astype(BF16)
    lo = (r1 - mid.astype(F32)).astype(BF16)
    return hi, mid, lo


def _dot(a, b):
    return jnp.dot(a, b, preferred_element_type=F32)


def _dot_nt(a, b):
    return lax.dot_general(a, b, (((1,), (1,)), ((), ())), preferred_element_type=F32)


def _dot_tn(a, b):
    return lax.dot_general(a, b, (((0,), (0,)), ((), ())), preferred_element_type=F32)


def _causal_conv(pre, prev8s, w, b):
    width = w.shape[0]
    rows, ch = pre.shape
    tiles_per_seg = rows // len(prev8s) // SUBLANES
    sub = lax.broadcasted_iota(jnp.int32, (1, SUBLANES, 1), 1)
    strips = []
    for c0 in range(0, ch, LANES):
        cs = slice(c0, c0 + LANES)
        x3 = pre[:, cs].reshape(rows // SUBLANES, SUBLANES, LANES)
        acc = b[:, cs].reshape(1, 1, LANES) + w[width - 1:width, cs].reshape(1, 1, LANES) * x3
        for j in range(1, width):
            rot = pltpu.roll(x3, j, axis=1)
            pieces = []
            for s, prev8 in enumerate(prev8s):
                pieces.append(pltpu.roll(prev8[:, cs].reshape(1, SUBLANES, LANES), j, axis=1))
                if tiles_per_seg > 1:
                    pieces.append(rot[s * tiles_per_seg:(s + 1) * tiles_per_seg - 1])
            prev = pieces[0] if len(pieces) == 1 else jnp.concatenate(pieces, axis=0)
            acc = acc + w[width - 1 - j:width - j, cs].reshape(1, 1, LANES) * jnp.where(sub < j, prev, rot)
        strips.append(acc.reshape(rows, LANES))
    return strips[0] if len(strips) == 1 else jnp.concatenate(strips, axis=1)


INPROJ_SUB = 256
INPROJ_CONV_SUB = 512
INPROJ_SHORT_TILE = 512


def _inproj_kernel(*refs, tm, seg, nt, has_c0):
    if has_c0:
        (x_ref, g_ref, wh_ref, wt_ref, wdt_ref, gb_ref, cos_ref, sin_ref, cw_ref, cbias_ref, c0_ref,
         u_ref, dt_ref, co_ref, h_ref, carry_scr, acc_scr) = refs
    else:
        (x_ref, g_ref, wh_ref, wt_ref, wdt_ref, gb_ref, cos_ref, sin_ref, cw_ref, cbias_ref,
         u_ref, dt_ref, co_ref, h_ref, carry_scr, acc_scr) = refs
        c0_ref = None
    i = pl.program_id(0)
    j = pl.program_id(1)
    sub = min(INPROJ_SUB, tm)
    nseg = tm // seg

    @pl.when(j == 0)
    def _():
        for r in range(tm // sub):
            rs = slice(r * sub, (r + 1) * sub)
            hb = _rms(x_ref[rs, :], g_ref[...]).astype(BF16)
            h_ref[rs, :] = hb
            dt_ref[rs, :] = _dot(hb, wdt_ref[...])

    def tiles(epilogue, w_ref):
        w = w_ref[...]
        for r in range(tm // sub):
            rs = slice(r * sub, (r + 1) * sub)
            u_ref[rs, :] = epilogue(_dot(h_ref[rs, :], w), rs).astype(BF16)

    def rotary(scale):
        def ep(acc, rs):
            cos = cos_ref[rs, :]
            sin = sin_ref[rs, :]
            half = RET_DK // 2
            out = []
            for h in range(U_TILE // RET_DK):
                x1 = acc[:, h * RET_DK:h * RET_DK + half]
                x2 = acc[:, h * RET_DK + half:(h + 1) * RET_DK]
                out += [(x1 * cos - x2 * sin) * scale, (x2 * cos + x1 * sin) * scale]
            return jnp.concatenate(out, axis=1)
        return ep

    is_silu = (j < COL_X // U_TILE) | ((j >= COL_G // U_TILE) & (j < COL_GATES // U_TILE))
    is_gate = (j >= COL_GATES // U_TILE) & (j < TILE_Q)
    is_conv = ((j >= COL_X // U_TILE) & (j < COL_V // U_TILE)) | (j == TILE_BC)
    is_plain = jnp.logical_not(is_silu | is_gate | is_conv | (j == TILE_Q) | (j == TILE_K))

    @pl.when(is_plain)
    def _():
        tiles(lambda acc, rs: acc, wt_ref)

    @pl.when(is_conv)
    def _():
        cblk = _conv_block(j)
        cw = cw_ref[...]
        cbias = cbias_ref[...]
        wh = wh_ref[...]
        if nseg == 1:
            carried = carry_scr[cblk]
            init = c0_ref[0] if has_c0 else jnp.zeros_like(carried)
            prev = jnp.where(i % nt == 0, init, carried)
            csub = min(INPROJ_CONV_SUB, tm)
            for r in range(tm // csub):
                rs = slice(r * csub, (r + 1) * csub)
                slot = (j + r) % 2
                acc_scr[slot] = _dot(h_ref[rs, :], wh)
                acc = acc_scr[slot]
                u_ref[rs, :] = _silu(_causal_conv(acc, [prev], cw, cbias)).astype(BF16)
                prev = acc[csub - SUBLANES:csub]
            carry_scr[cblk] = prev
            co_ref[0] = prev
        else:
            acc = _dot(h_ref[...], wh)
            zero8 = jnp.zeros((SUBLANES, U_TILE), F32)
            prev8s = [c0_ref[s] if has_c0 else zero8 for s in range(nseg)]
            u_ref[...] = _silu(_causal_conv(acc, prev8s, cw, cbias)).astype(BF16)
            for s in range(nseg):
                co_ref[s] = acc[(s + 1) * seg - SUBLANES:(s + 1) * seg]

    @pl.when(is_silu & (j < TILE_V))
    def _():
        tiles(lambda acc, rs: _silu(acc), wh_ref)

    @pl.when(is_silu & (j >= TILE_V))
    def _():
        tiles(lambda acc, rs: _silu(acc), wt_ref)

    @pl.when(is_gate)
    def _():
        tiles(lambda acc, rs: _sigmoid(acc + gb_ref[...]), wt_ref)

    @pl.when(j == TILE_Q)
    def _():
        tiles(rotary(1.0), wt_ref)

    @pl.when(j == TILE_K)
    def _():
        tiles(rotary(RET_DK ** -0.5), wt_ref)


def _conv_block(j):
    return jnp.where(j >= TILE_BC, 2, jnp.where(j > COL_X // U_TILE, 1, 0))


def _in_proj(x2d, prm, cos, sin, conv0, nb, t, tm):
    n = x2d.shape[0]
    npos = cos.shape[0] // tm
    seg = min(t, tm)
    nseg = tm // seg
    nt = t // seg
    assert nseg == 1 or tm <= INPROJ_SHORT_TILE
    gate_tile0 = COL_GATES // U_TILE
    gb_map = lambda i, j: (0, jnp.clip(j - gate_tile0, 0, 2 * D_MODEL // U_TILE - 1))
    cmap = lambda i, j: (0, _conv_block(j))
    tail_spec = pl.BlockSpec((nseg, SUBLANES, U_TILE), lambda i, j: (i // nt, 0, _conv_block(j)))
    operands = [x2d, prm["norm1_g"], prm["w_head"], prm["w_tail"], prm["w_dt"], prm["gate_b"], cos, sin,
                prm["ssd_conv_w"], prm["ssd_conv_b"]]
    in_specs = [
        pl.BlockSpec((tm, D_MODEL), lambda i, j: (i, 0)),
        pl.BlockSpec((1, D_MODEL), lambda i, j: (0, 0)),
        pl.BlockSpec((D_MODEL, U_TILE), lambda i, j: (0, jnp.minimum(j, TILE_V))),
        pl.BlockSpec((D_MODEL, U_TILE), lambda i, j: (0, jnp.where(
            j < TILE_V, 2, jnp.where(j < TILE_Q, j - 2, jnp.where(j == TILE_Q, 0, 1))))),
        pl.BlockSpec((D_MODEL, DT_PAD), lambda i, j: (0, 0)),
        pl.BlockSpec((1, U_TILE), gb_map),
        pl.BlockSpec((tm, RET_DK // 2), lambda i, j: (i % npos, 0)),
        pl.BlockSpec((tm, RET_DK // 2), lambda i, j: (i % npos, 0)),
        pl.BlockSpec((SSD_CONV, U_TILE), cmap),
        pl.BlockSpec((1, U_TILE), cmap),
    ]
    if conv0 is not None:
        operands.append(conv0)
        in_specs.append(tail_spec)
    u, dt_raw, tails = pl.pallas_call(
        functools.partial(_inproj_kernel, tm=tm, seg=seg, nt=nt, has_c0=conv0 is not None),
        grid=(n // tm, U_COLS // U_TILE),
        in_specs=in_specs,
        out_specs=[
            pl.BlockSpec((tm, U_TILE), lambda i, j: (i, j)),
            pl.BlockSpec((tm, DT_PAD), lambda i, j: (i, 0)),
            pl.BlockSpec((nseg, SUBLANES, U_TILE), lambda i, j: (i, 0, _conv_block(j))),
        ],
        out_shape=[
            jax.ShapeDtypeStruct((n, U_COLS), BF16),
            jax.ShapeDtypeStruct((n, DT_PAD), F32),
            jax.ShapeDtypeStruct((nb * nt, SUBLANES, SSD_D_INNER + 2 * SSD_BC), F32),
        ],
        scratch_shapes=[pltpu.VMEM((tm, D_MODEL), BF16), pltpu.VMEM((3, SUBLANES, U_TILE), F32),
                        pltpu.VMEM((2, min(INPROJ_CONV_SUB, tm), U_TILE), F32)],
        compiler_params=_cparams(("arbitrary", "arbitrary")),
        name="in_proj",
    )(*operands)
    xbc = SSD_D_INNER + 2 * SSD_BC
    return u, dt_raw, tails.reshape(nb, nt, SUBLANES, xbc)[:, nt - 1]


def _ssd_kernel(*refs, L, Tv, has_s0):
    if has_s0:
        (x_ref, b_ref, c_ref, z_ref, dt_ref, dtb_ref, alog_ref, dsk_ref, ng_ref, e64_ref, s0_ref,
         y_ref, so_ref, s_scr, *pad_scr) = refs
    else:
        (x_ref, b_ref, c_ref, z_ref, dt_ref, dtb_ref, alog_ref, dsk_ref, ng_ref, e64_ref,
         y_ref, so_ref, s_scr, *pad_scr) = refs
    c = pl.program_id(1)
    nc = pl.num_programs(1)
    Q = LANES
    nq = L // Q

    @pl.when(c == 0)
    def _():
        if has_s0:
            s_scr[...] = s0_ref[0].reshape(SSD_D_INNER, SSD_STATE)
        else:
            s_scr[...] = jnp.zeros(s_scr.shape, F32)

    if Tv < L:
        for scr, ref in zip(pad_scr, (x_ref, b_ref, c_ref, z_ref, dt_ref)):
            scr[...] = jnp.zeros(scr.shape, scr.dtype)
            scr[0:Tv, :] = ref[...]
        x_in, b_in, c_in, z_in, dt_in = pad_scr
    else:
        x_in, b_in, c_in, z_in, dt_in = x_ref, b_ref, c_ref, z_ref, dt_ref

    rows = lax.broadcasted_iota(jnp.int32, (L, 1), 0)
    dtv = dt_in[...] + dtb_ref[...]
    dt = jnp.maximum(dtv, 0.0) + jnp.log(1.0 + jnp.exp(-jnp.abs(dtv)))
    if Tv < L:
        dt = jnp.where(rows < Tv, dt, 0.0)
    dA = dt * (-jnp.exp(alog_ref[...]))

    causal = (lax.broadcasted_iota(jnp.int32, (L, L), 0) >= lax.broadcasted_iota(jnp.int32, (L, L), 1))
    tri = jnp.where(causal, 1.0, 0.0).astype(BF16)
    hi, mid, lo = _split3(dA)
    cum = _dot(tri, hi) + _dot(tri, mid) + _dot(tri, lo)
    dec_tot = jnp.exp(cum[Tv - 1:Tv, :])
    c2 = cum * LOG2E
    r_t = (c2 - jnp.log2(dt)).T
    tri_q = (lax.broadcasted_iota(jnp.int32, (Q, Q), 0) >= lax.broadcasted_iota(jnp.int32, (Q, Q), 1))

    lane = lax.broadcasted_iota(jnp.int32, (1, LANES), 1)

    def pack3(v):
        r1 = v - v.astype(BF16).astype(F32)
        r2 = r1 - r1.astype(BF16).astype(F32)
        return jnp.where(lane < SSD_HEADS, v,
                         jnp.where(lane < 2 * SSD_HEADS, pltpu.roll(r1, SSD_HEADS, axis=1),
                                   pltpu.roll(r2, 2 * SSD_HEADS, axis=1))).astype(BF16)

    c2p = pack3(c2)
    dtp = pack3(dt)

    lo_half = lane < SSD_HEAD_DIM
    P2 = 2 * SSD_HEAD_DIM

    for g in range(SSD_GROUPS):
        gx = slice(g * SSD_GROUP_CH, (g + 1) * SSD_GROUP_CH)
        gn = slice(g * SSD_STATE, (g + 1) * SSD_STATE)
        c2_x64 = _dot(c2p, e64_ref[:, gx])
        dt_x64 = _dot(dtp, e64_ref[:, gx])
        ecum_x = jnp.exp2(c2_x64)
        todt_x = jnp.exp2(c2_x64[Tv - 1:Tv, :] - c2_x64) * dt_x64
        xcb = x_in[:, gx]
        bcb = b_in[:, gn]
        ccb = c_in[:, gn]
        xc = xcb.astype(F32)
        cb = _dot_nt(ccb, bcb)
        s_old = s_scr[gx, :]
        y_inter = _dot_nt(ccb, s_old.astype(BF16))

        y_parts = []
        xw_parts = []
        for pr in range(HEADS_PER_GROUP // 2):
            heads = [g * HEADS_PER_GROUP + 2 * pr + k for k in range(2)]
            ps = slice(pr * P2, (pr + 1) * P2)
            x_pair = xc[:, ps]
            x_pair_b = xcb[:, ps]
            zero = jnp.zeros_like(x_pair_b)
            x_a = jnp.where(lo_half, x_pair_b, zero)
            x_b = jnp.where(lo_half, zero, x_pair_b)
            y_rows = []
            for i in range(nq):
                ri = slice(i * Q, (i + 1) * Q)
                lhs, rhs = [], []
                for h, xh in zip(heads, (x_a, x_b)):
                    for jq in range(i + 1):
                        rj = slice(jq * Q, (jq + 1) * Q)
                        e = jnp.exp2(c2[ri, h:h + 1] - r_t[h:h + 1, rj])
                        if jq == i:
                            e = jnp.where(tri_q, e, 0.0)
                        lhs.append((e * cb[ri, rj]).astype(BF16))
                    rhs.append(xh[0:(i + 1) * Q])
                y_rows.append(_dot(jnp.concatenate(lhs, axis=1), jnp.concatenate(rhs, axis=0)))
            y_pair = y_rows[0] if nq == 1 else jnp.concatenate(y_rows, axis=0)
            pg = slice(g * SSD_GROUP_CH + pr * P2, g * SSD_GROUP_CH + (pr + 1) * P2)
            y_pair = y_pair + y_inter[:, ps] * ecum_x[:, ps]
            y_pair = y_pair + x_pair * dsk_ref[:, pg]
            y_parts.append(y_pair)
            xw_parts.append((x_pair * todt_x[:, ps]).astype(BF16))

        y = jnp.concatenate(y_parts, axis=1)
        xw = jnp.concatenate(xw_parts, axis=1)

        dec_rows = [jnp.broadcast_to(dec_tot[:, g * HEADS_PER_GROUP + k:g * HEADS_PER_GROUP + k + 1],
                                     (SSD_HEAD_DIM, 1)) for k in range(HEADS_PER_GROUP)]
        s_new = s_old * jnp.concatenate(dec_rows, axis=0) + _dot_tn(xw, bcb)
        s_scr[gx, :] = s_new

        yz = y * z_in[:, gx].astype(F32)
        yn = yz * lax.rsqrt(jnp.mean(yz * yz, axis=-1, keepdims=True) + EPS) * ng_ref[:, gx]
        y_ref[:, gx] = yn[0:Tv].astype(BF16)

    @pl.when(c == nc - 1)
    def _():
        so_ref[0] = s_scr[...].reshape(SSD_HEADS, SSD_HEAD_DIM, SSD_STATE)


def _ssd(u, dt_raw, prm, s0, nb, t, L, Tv):
    nc = t // Tv
    rb = lambda b, c: b * nc + c
    full = lambda w: pl.BlockSpec(w.shape, lambda b, c: (0,) * w.ndim)
    state_spec = pl.BlockSpec((1, SSD_HEADS, SSD_HEAD_DIM, SSD_STATE), lambda b, c: (b, 0, 0, 0))
    names = ["dt_bias", "a_log", "d_skip", "ssd_norm_g", "expand64"]
    operands = [u, u, u, u, dt_raw] + [prm[k] for k in names]
    in_specs = [
        pl.BlockSpec((Tv, SSD_D_INNER), lambda b, c: (rb(b, c), COL_X // SSD_D_INNER)),
        pl.BlockSpec((Tv, SSD_BC), lambda b, c: (rb(b, c), COL_B // SSD_BC)),
        pl.BlockSpec((Tv, SSD_BC), lambda b, c: (rb(b, c), COL_C // SSD_BC)),
        pl.BlockSpec((Tv, SSD_D_INNER), lambda b, c: (rb(b, c), COL_Z // SSD_D_INNER)),
        pl.BlockSpec((Tv, DT_PAD), lambda b, c: (rb(b, c), 0)),
    ] + [full(prm[k]) for k in names]
    if s0 is not None:
        operands.append(s0)
        in_specs.append(state_spec)
    scratch = [pltpu.VMEM((SSD_D_INNER, SSD_STATE), F32)]
    if Tv < L:
        scratch += [
            pltpu.VMEM((L, SSD_D_INNER), BF16), pltpu.VMEM((L, SSD_BC), BF16),
            pltpu.VMEM((L, SSD_BC), BF16), pltpu.VMEM((L, SSD_D_INNER), BF16),
            pltpu.VMEM((L, DT_PAD), F32),
        ]
    return pl.pallas_call(
        functools.partial(_ssd_kernel, L=L, Tv=Tv, has_s0=s0 is not None),
        grid=(nb, nc),
        in_specs=in_specs,
        out_specs=[pl.BlockSpec((Tv, SSD_D_INNER), lambda b, c: (rb(b, c), 0)), state_spec],
        out_shape=[jax.ShapeDtypeStruct((nb * t, SSD_D_INNER), BF16),
                   jax.ShapeDtypeStruct((nb, SSD_HEADS, SSD_HEAD_DIM, SSD_STATE), F32)],
        scratch_shapes=scratch,
        compiler_params=_cparams(("parallel", "arbitrary")),
        name="ssd",
    )(*operands)


def _ret_kernel(*refs, L, Tv, has_s0):
    if has_s0:
        (q_ref, k_ref, v_ref, g_ref, lg_ref, ng_ref, s0_ref,
         y_ref, so_ref, s_scr, dm_scr, cross_scr, kdec_scr, *pad_scr) = refs
    else:
        (q_ref, k_ref, v_ref, g_ref, lg_ref, ng_ref,
         y_ref, so_ref, s_scr, dm_scr, cross_scr, kdec_scr, *pad_scr) = refs
    c = pl.program_id(1)
    nc = pl.num_programs(1)

    @pl.when(c == 0)
    def _():
        if has_s0:
            s_scr[...] = s0_ref[0]
        else:
            s_scr[...] = jnp.zeros(s_scr.shape, F32)
        rowf = lax.broadcasted_iota(jnp.int32, (L, 1), 0).astype(F32)
        colf = lax.broadcasted_iota(jnp.int32, (1, L), 1).astype(F32)
        diff = rowf - colf
        for h in range(RET_HEADS):
            lg = lg_ref[h][:, 0:1]
            dm_scr[h] = jnp.where(diff >= 0.0, jnp.exp(jnp.maximum(diff, 0.0) * lg), 0.0)
            cross_scr[h] = jnp.broadcast_to(jnp.exp((rowf + 1.0) * lg), (L, LANES))
            kdec = jnp.exp((Tv - 1.0 - rowf) * lg)
            if Tv < L:
                kdec = jnp.where(rowf < Tv, kdec, 0.0)
            kdec_scr[h] = jnp.broadcast_to(kdec, (L, LANES))

    if Tv < L:
        for scr, ref in zip(pad_scr, (q_ref, k_ref, v_ref, g_ref)):
            scr[...] = jnp.zeros(scr.shape, scr.dtype)
            scr[0:Tv, :] = ref[...]
        q_in, k_in, v_in, g_in = pad_scr
    else:
        q_in, k_in, v_in, g_in = q_ref, k_ref, v_ref, g_ref

    for h in range(RET_HEADS):
        ks = slice(h * RET_DK, (h + 1) * RET_DK)
        vs = slice(h * RET_DV, (h + 1) * RET_DV)
        qh = q_in[:, ks]
        kh = k_in[:, ks]
        vh = v_in[:, vs]
        s_old = s_scr[h]
        sc = (_dot_nt(qh, kh) * dm_scr[h]).astype(BF16)
        cross = jnp.concatenate([cross_scr[h]] * (RET_DV // LANES), axis=1)
        o = _dot(sc, vh) + _dot(qh, s_old.astype(BF16)) * cross
        kdec = jnp.concatenate([kdec_scr[h]] * (RET_DK // LANES), axis=1)
        kd = (kh.astype(F32) * kdec).astype(BF16)
        s_new = s_old * jnp.exp(Tv * lg_ref[h][:, 0:1]) + _dot_tn(kd, vh)
        s_scr[h] = s_new

        mu = jnp.mean(o, axis=-1, keepdims=True)
        d = o - mu
        var = jnp.mean(d * d, axis=-1, keepdims=True)
        y = d * lax.rsqrt(var + EPS) * ng_ref[:, vs] * g_in[:, vs].astype(F32)
        y_ref[:, vs] = y[0:Tv].astype(BF16)

    @pl.when(c == nc - 1)
    def _():
        so_ref[0] = s_scr[...]


def _ret(u, lg, ng, s0, nb, t, L, Tv):
    nc = t // Tv
    rb = lambda b, c: b * nc + c
    state_spec = pl.BlockSpec((1, RET_HEADS, RET_DK, RET_DV), lambda b, c: (b, 0, 0, 0))
    in_specs = [
        pl.BlockSpec((Tv, RET_QK), lambda b, c: (rb(b, c), COL_Q // RET_QK)),
        pl.BlockSpec((Tv, RET_QK), lambda b, c: (rb(b, c), COL_K // RET_QK)),
        pl.BlockSpec((Tv, RET_V), lambda b, c: (rb(b, c), COL_V // RET_V)),
        pl.BlockSpec((Tv, RET_V), lambda b, c: (rb(b, c), COL_G // RET_V)),
        pl.BlockSpec(lg.shape, lambda b, c: (0, 0, 0)),
        pl.BlockSpec((1, RET_V), lambda b, c: (0, 0)),
    ]
    operands = [u, u, u, u, lg, ng]
    if s0 is not None:
        operands.append(s0)
        in_specs.append(state_spec)
    out_specs = [pl.BlockSpec((Tv, RET_V), lambda b, c: (rb(b, c), 0)), state_spec]
    out_shape = [
        jax.ShapeDtypeStruct((nb * t, RET_V), BF16),
        jax.ShapeDtypeStruct((nb, RET_HEADS, RET_DK, RET_DV), F32),
    ]
    scratch = [
        pltpu.VMEM((RET_HEADS, RET_DK, RET_DV), F32),
        pltpu.VMEM((RET_HEADS, L, L), F32),
        pltpu.VMEM((RET_HEADS, L, LANES), F32),
        pltpu.VMEM((RET_HEADS, L, LANES), F32),
    ]
    if Tv < L:
        scratch += [
            pltpu.VMEM((L, RET_QK), BF16), pltpu.VMEM((L, RET_QK), BF16),
            pltpu.VMEM((L, RET_V), BF16), pltpu.VMEM((L, RET_V), BF16),
        ]
    return pl.pallas_call(
        functools.partial(_ret_kernel, L=L, Tv=Tv, has_s0=s0 is not None),
        grid=(nb, nc),
        in_specs=in_specs, out_specs=out_specs, out_shape=out_shape,
        scratch_shapes=scratch,
        compiler_params=_cparams(("parallel", "arbitrary")),
        name="retention",
    )(*operands)


def _merge_kernel(x_ref, ys_ref, yr_ref, gt_ref, ws_ref, wr_ref, wo_ref, o_ref):
    a = _dot(ys_ref[...], ws_ref[...])
    b = _dot(yr_ref[...], wr_ref[...])
    mix = gt_ref[:, :D_MODEL].astype(F32) * a + gt_ref[:, D_MODEL:].astype(F32) * b
    o_ref[...] = x_ref[...] + _dot(mix.astype(BF16), wo_ref[...])


def _merge(x2d, y_ssd, y_ret, u, w_s, w_r, w_o, tm):
    n = x2d.shape[0]
    return pl.pallas_call(
        _merge_kernel,
        grid=(n // tm,),
        in_specs=[
            pl.BlockSpec((tm, D_MODEL), lambda i: (i, 0)),
            pl.BlockSpec((tm, SSD_D_INNER), lambda i: (i, 0)),
            pl.BlockSpec((tm, RET_V), lambda i: (i, 0)),
            pl.BlockSpec((tm, 2 * D_MODEL), lambda i: (i, COL_GATES // (2 * D_MODEL))),
            _resident(w_s.shape), _resident(w_r.shape), _resident(w_o.shape),
        ],
        out_specs=pl.BlockSpec((tm, D_MODEL), lambda i: (i, 0)),
        out_shape=jax.ShapeDtypeStruct((n, D_MODEL), F32),
        compiler_params=_cparams(("parallel",)),
        name="merge",
    )(x2d, y_ssd, y_ret, u, w_s, w_r, w_o)


FFN_CHUNK = 256
FFN_NCHUNK = D_FF // FFN_CHUNK


def _ffn_kernel(*refs, tm, seg, has_c0):
    if has_c0:
        (x_ref, p_ref, n2_ref, wup_ref, cw_ref, cb_ref, wdn_ref, pg_ref, wpg_ref, wpp_ref, fg_ref, c0_ref,
         y_ref, co_ref, act_scr, carry_scr) = refs
    else:
        (x_ref, p_ref, n2_ref, wup_ref, cw_ref, cb_ref, wdn_ref, pg_ref, wpg_ref, wpp_ref, fg_ref,
         y_ref, co_ref, act_scr, carry_scr) = refs
    nseg = tm // seg
    if nseg == 1:
        i = pl.program_id(1)

        @pl.when(i == 0)
        def _():
            if has_c0:
                carry_scr[...] = c0_ref[...]
            else:
                carry_scr[...] = jnp.zeros(carry_scr.shape, F32)
        prev_ref = carry_scr
    else:
        assert has_c0
        prev_ref = c0_ref

    x1 = x_ref[...]
    h2 = _rms(x1, n2_ref[...]).astype(BF16)

    for cc in range(FFN_NCHUNK):
        halves = []
        for base in (0, D_FF):
            sl = slice(base + cc * FFN_CHUNK, base + (cc + 1) * FFN_CHUNK)
            up = _dot(h2, wup_ref[:, sl])
            prev8s = [prev_ref[s, :, sl] for s in range(nseg)]
            halves.append(_causal_conv(up, prev8s, cw_ref[:, sl], cb_ref[:, sl]))
            for s in range(nseg):
                tail = up[(s + 1) * seg - SUBLANES:(s + 1) * seg]
                if nseg == 1:
                    carry_scr[s, :, sl] = tail
                else:
                    co_ref[s, :, sl] = tail
        a, b = halves
        gelu = 0.5 * a * (1.0 + lax.erf(a * (2.0 ** -0.5)))
        act_scr[:, cc * FFN_CHUNK:(cc + 1) * FFN_CHUNK] = (gelu * b).astype(BF16)

    x2 = x1 + _dot(act_scr[...], wdn_ref[...])
    hg = _rms(x2, pg_ref[...]).astype(BF16)
    gate = _sigmoid(_dot(hg, wpg_ref[...]))
    x3 = x2 + gate * _dot(p_ref[...].astype(BF16), wpp_ref[...])
    y_ref[...] = _rms(x3, fg_ref[...])

    if nseg == 1:
        @pl.when(i == pl.num_programs(1) - 1)
        def _():
            co_ref[...] = carry_scr[...]


def _ffn(x1, p2d, prm, tails, nb, t, tm):
    n = nb * t
    seg = min(t, tm)
    nseg = tm // seg
    weights = [prm[k] for k in ("norm2_g", "w_up", "ffn_conv_w", "ffn_conv_b", "w_down", "ple_norm_g",
                                "w_ple_gate", "w_ple_proj", "final_norm_g")]
    wspecs = [_resident(w.shape) for w in weights]
    nt = t // seg
    grid = (n // (tm * nt), nt)
    row = lambda b, i: (b * nt + i, 0)
    tail_spec = pl.BlockSpec((nseg, SUBLANES, 2 * D_FF), lambda b, i: (b, 0, 0))
    operands = [x1, p2d, *weights]
    in_specs = [pl.BlockSpec((tm, D_MODEL), row), pl.BlockSpec((tm, D_PLE), row)] + wspecs
    if tails is not None:
        operands.append(tails)
        in_specs.append(tail_spec)
    return pl.pallas_call(
        functools.partial(_ffn_kernel, tm=tm, seg=seg, has_c0=tails is not None),
        grid=grid,
        in_specs=in_specs,
        out_specs=[pl.BlockSpec((tm, D_MODEL), row), tail_spec],
        out_shape=[jax.ShapeDtypeStruct((n, D_MODEL), F32),
                   jax.ShapeDtypeStruct((nb, SUBLANES, 2 * D_FF), F32)],
        scratch_shapes=[pltpu.VMEM((tm, D_FF), BF16), pltpu.VMEM((1, SUBLANES, 2 * D_FF), F32)],
        compiler_params=_cparams(("parallel", "arbitrary")),
        name="ffn",
    )(*operands)


def _prep_params(norm1_g, w_in, ssd_conv_w, ssd_conv_b, dt_bias, a_log, d_skip, ssd_norm_g, w_br_ssd,
                 ret_norm_g, w_br_ret, gate_b, w_out, norm2_g, w_up, ffn_conv_w, ffn_conv_b, w_down,
                 ple_norm_g, w_ple_gate, w_ple_proj, final_norm_g):
    o = {}
    n_head = 2 * SSD_D_INNER + 2 * SSD_BC
    o["w_head"] = w_in[:, :n_head].astype(BF16)
    o["w_tail"] = w_in[:, n_head + SSD_HEADS:].astype(BF16)
    o["w_dt"] = jnp.pad(w_in[:, n_head:n_head + SSD_HEADS], ((0, 0), (0, DT_PAD - SSD_HEADS))).astype(BF16)
    o["norm1_g"] = norm1_g.reshape(1, -1)
    o["ssd_conv_w"] = ssd_conv_w
    o["ssd_conv_b"] = ssd_conv_b.reshape(1, -1)
    o["dt_bias"] = jnp.pad(dt_bias.reshape(1, -1), ((0, 0), (0, DT_PAD - SSD_HEADS)))
    o["a_log"] = jnp.pad(a_log.reshape(1, -1), ((0, 0), (0, DT_PAD - SSD_HEADS)))
    o["d_skip"] = jnp.repeat(d_skip, SSD_HEAD_DIM).reshape(1, -1)
    o["ssd_norm_g"] = ssd_norm_g.reshape(1, -1)
    o["w_br_ssd"] = w_br_ssd.astype(BF16)
    o["ret_norm_g"] = ret_norm_g.reshape(1, -1)
    o["w_br_ret"] = w_br_ret.astype(BF16)
    o["gate_b"] = gate_b.reshape(1, -1)
    o["w_out"] = w_out.astype(BF16)
    o["norm2_g"] = norm2_g.reshape(1, -1)
    o["w_up"] = w_up.astype(BF16)
    o["ffn_conv_w"] = ffn_conv_w
    o["ffn_conv_b"] = ffn_conv_b.reshape(1, -1)
    o["w_down"] = w_down.astype(BF16)
    o["ple_norm_g"] = ple_norm_g.reshape(1, -1)
    o["w_ple_gate"] = w_ple_gate.astype(BF16)
    o["w_ple_proj"] = w_ple_proj.astype(BF16)
    o["final_norm_g"] = final_norm_g.reshape(1, -1)
    piece_head = np.arange(LANES) % SSD_HEADS
    used = np.arange(LANES) < 3 * SSD_HEADS
    col_head = np.arange(SSD_D_INNER) // SSD_HEAD_DIM
    o["expand64"] = jnp.asarray((piece_head[:, None] == col_head[None, :]) & used[:, None], BF16)
    lg = np.log1p(-np.power(2.0, -5.0 - np.arange(RET_HEADS)))
    o["log_gamma"] = jnp.asarray(np.broadcast_to(lg[:, None, None], (RET_HEADS, 1, LANES)), F32)
    return o


def _rope_tables(pos0, t):
    half = RET_DK // 2
    inv = np.power(ROPE_BASE, -np.arange(half) / half)
    ang = (pos0 + np.arange(t))[:, None] * inv[None, :]
    return np.cos(ang).astype(np.float32), np.sin(ang).astype(np.float32)


def _pad_tail(buf):
    return jnp.pad(buf, ((0, 0), (SUBLANES - buf.shape[1], 0), (0, 0)))


def _tiles(nb, t):
    n = nb * t
    tm_in = min(INPROJ_TILE, t) if t >= INPROJ_SHORT_TILE else min(INPROJ_SHORT_TILE, n)
    tm_merge = min(MERGE_TILE, n)
    tm_ffn = min(FFN_TILE, t) if t >= FFN_SHORT_TILE else min(FFN_SHORT_TILE, n)
    tv = min(t, MIXER_CHUNK)
    return tm_in, tm_merge, tm_ffn, tv, max(tv, MIXER_MIN_ROWS)


def _group(x, p, pos0, states, prm):
    nb, t, _ = x.shape
    n = nb * t
    x2d = x.reshape(n, D_MODEL)
    p2d = p.reshape(n, D_PLE)
    tm_in, tm_merge, tm_ffn, Tv, L = _tiles(nb, t)
    if states is None:
        conv0 = ssd_s = ret_s = ffn0 = None
    else:
        conv_buf, ssd_s, ret_s, ffn_buf = states
        conv0, ffn0 = _pad_tail(conv_buf), _pad_tail(ffn_buf)

    cos, sin = _rope_tables(pos0, t)
    if t < tm_in:
        cos, sin = np.tile(cos, (tm_in // t, 1)), np.tile(sin, (tm_in // t, 1))
    u, dt_raw, conv_tail = _in_proj(x2d, prm, jnp.asarray(cos), jnp.asarray(sin), conv0, nb, t, tm_in)
    conv_new = conv_tail[:, SUBLANES - (SSD_CONV - 1):, :]

    y_ssd, ssd_new = _ssd(u, dt_raw, prm, ssd_s, nb, t, L, Tv)
    y_ret, ret_new = _ret(u, prm["log_gamma"], prm["ret_norm_g"], ret_s, nb, t, L, Tv)
    x1 = _merge(x2d, y_ssd, y_ret, u, prm["w_br_ssd"], prm["w_br_ret"], prm["w_out"], tm_merge)
    y, ffn_tail = _ffn(x1, p2d, prm, ffn0, nb, t, tm_ffn)
    ffn_new = ffn_tail[:, SUBLANES - (FFN_CONV - 1):, :]
    return y.reshape(nb, t, D_MODEL), conv_new, ssd_new, ret_new, ffn_new


def kernel(x_prompt, x_sample, p_prompt, p_sample, state_ssd_conv, state_ssd, state_ret, state_ffn_conv,
           norm1_g, w_in, ssd_conv_w, ssd_conv_b, dt_bias, a_log, d_skip, ssd_norm_g, w_br_ssd,
           ret_norm_g, w_br_ret, gate_b, w_out, norm2_g, w_up, ffn_conv_w, ffn_conv_b, w_down,
           ple_norm_g, w_ple_gate, w_ple_proj, final_norm_g):
    assert norm1_g.shape[0] == 1, "single-layer model"
    prm = _prep_params(norm1_g[0], w_in[0], ssd_conv_w[0], ssd_conv_b[0], dt_bias[0], a_log[0], d_skip[0],
                       ssd_norm_g[0], w_br_ssd[0], ret_norm_g[0], w_br_ret[0], gate_b[0], w_out[0],
                       norm2_g[0], w_up[0], ffn_conv_w[0], ffn_conv_b[0], w_down[0], ple_norm_g[0],
                       w_ple_gate[0], w_ple_proj[0], final_norm_g)
    yp, cp, sp, rp, fp = _group(x_prompt, p_prompt[0], 0, None, prm)
    ys, cs, ss, rs, fs = _group(x_sample, p_sample[0], PAST_LEN,
                                (state_ssd_conv[0], state_ssd[0], state_ret[0], state_ffn_conv[0]), prm)
    return (yp, ys, cp[None], sp[None], rp[None], fp[None], cs[None], ss[None], rs[None], fs[None])
```
